```python
import math
import jax, jax.numpy as jnp
from jax import lax
import numpy as np

D_MODEL = 1024
BATCH = 32
SEQ = 256
DEPTH = 4
DEC_BATCH = 2
DEC_SEQ = 1024
PAST_LEN = 512

GRID_W = 64
N_MIXERS = 2
N_RET = (DEPTH + 1) // 2
N_DEL = DEPTH // 2
CHUNK = 64
EPS = 1e-6

RET_HEADS = 4
RET_DK = D_MODEL // RET_HEADS
RET_DV = 2 * RET_DK
RET_QK = RET_HEADS * RET_DK
RET_V = RET_HEADS * RET_DV
RET_IN = 2 * RET_QK + 2 * RET_V
ROPE_BASE = 10000.0

DEL_HEADS = 8
DEL_DK = D_MODEL // DEL_HEADS
DEL_DV = 2 * DEL_DK
DEL_QK = DEL_HEADS * DEL_DK
DEL_V = DEL_HEADS * DEL_DV
DEL_CONV = 2 * DEL_QK + DEL_V
DEL_IN = DEL_CONV + DEL_V + 4 * DEL_HEADS
CONV_W = 3

kernel_name = "bidir_retention_gated_deltanet_prefix_dit"

F32 = jnp.float32


def _rmsnorm_f32(x, w):
    xf = x.astype(F32)
    return xf * lax.rsqrt(jnp.mean(jnp.square(xf), -1, keepdims=True) + EPS) * w.astype(F32)


def _flip(a):
    return jnp.flip(a, axis=1)


def _axial_rope(x):
    B, T, H, dk = x.shape
    rows = T // GRID_W
    r = jnp.broadcast_to(jnp.arange(rows)[:, None], (rows, GRID_W)).reshape(T).astype(F32)
    col = jnp.broadcast_to(jnp.arange(GRID_W)[None, :], (rows, GRID_W)).reshape(T).astype(F32)
    n_pairs = dk // 4
    freqs = ROPE_BASE ** (-jnp.arange(n_pairs, dtype=F32) / n_pairs)
    ang = jnp.concatenate([r[:, None] * freqs, col[:, None] * freqs], -1)
    cos = jnp.cos(ang)[:, None, :]
    sin = jnp.sin(ang)[:, None, :]
    x1, x2 = x[..., : dk // 2], x[..., dk // 2:]
    return jnp.concatenate([x1 * cos - x2 * sin, x1 * sin + x2 * cos], -1)


def _to_chunks(a):
    B, T, H, d = a.shape
    return a.reshape(B, T // CHUNK, CHUNK, H, d).transpose(1, 0, 3, 2, 4)


def _from_chunks(o):
    n, B, H, C, d = o.shape
    return o.transpose(1, 0, 3, 2, 4).reshape(B, n * C, H, d)


def _retention_scan(q, k, v, log_gamma, s0):
    idx = jnp.arange(CHUNK, dtype=F32)
    diff = idx[:, None] - idx[None, :]
    causal = diff >= 0
    lg = log_gamma[:, None, None]
    dmat = jnp.where(causal, jnp.exp(lg * jnp.where(causal, diff, 0.0)), 0.0)
    xi = jnp.exp(log_gamma[:, None] * (idx + 1.0))[..., None]
    zeta = jnp.exp(log_gamma[:, None] * (CHUNK - 1.0 - idx))[..., None]
    chunk_decay = jnp.exp(log_gamma * CHUNK)[:, None, None]

    def step(s, inp):
        qi, ki, vi = inp
        scores = jnp.einsum('bhid,bhjd->bhij', qi, ki) * dmat
        inner = jnp.einsum('bhij,bhjv->bhiv', scores, vi)
        cross = jnp.einsum('bhid,bhdv->bhiv', qi, s) * xi
        s_new = s * chunk_decay + jnp.einsum('bhjd,bhjv->bhdv', ki * zeta, vi)
        return s_new, inner + cross

    s_fin, o = lax.scan(step, s0, (_to_chunks(q), _to_chunks(k), _to_chunks(v)))
    return _from_chunks(o), s_fin


def _gated_delta_scan(q, k, v, beta, g, s0):
    B, T, H, _ = q.shape
    n = T // CHUNK
    bc = beta.reshape(B, n, CHUNK, H).transpose(1, 0, 3, 2)
    gc = g.reshape(B, n, CHUNK, H).transpose(1, 0, 3, 2)
    idx = jnp.arange(CHUNK)
    tril = idx[:, None] >= idx[None, :]
    strict = idx[:, None] > idx[None, :]
    eye = jnp.eye(CHUNK, dtype=F32)

    def step(s, inp):
        qi, ki, vi, bi, gi = inp
        G = jnp.cumsum(gi, axis=-1)
        gdiff = G[..., :, None] - G[..., None, :]
        L = jnp.where(tril, jnp.exp(jnp.where(tril, gdiff, 0.0)), 0.0)
        kk = jnp.einsum('bhid,bhjd->bhij', ki, ki)
        A = jnp.where(strict, bi[..., :, None] * kk * L, 0.0)
        Tm = lax.linalg.triangular_solve(eye + A, jnp.broadcast_to(eye, A.shape),
                                         left_side=True, lower=True, unit_diagonal=True)
        eG = jnp.exp(G)
        u = jnp.einsum('bhij,bhjv->bhiv', Tm, vi * bi[..., None])
        w = jnp.einsum('bhij,bhjd->bhid', Tm, ki * (bi * eG)[..., None])
        v_new = u - jnp.einsum('bhid,bhdv->bhiv', w, s)
        qk = jnp.einsum('bhid,bhjd->bhij', qi, ki) * L
        o = jnp.einsum('bhid,bhdv->bhiv', qi * eG[..., None], s) + jnp.einsum('bhij,bhjv->bhiv', qk, v_new)
        G_last = G[..., -1:]
        s_new = s * jnp.exp(G_last)[..., None] + jnp.einsum(
            'bhjd,bhjv->bhdv', ki * jnp.exp(G_last - G)[..., None], v_new)
        return s_new, o

    s_fin, o = lax.scan(step, s0, (_to_chunks(q), _to_chunks(k), _to_chunks(v), bc, gc))
    return _from_chunks(o), s_fin


def _retention_branch(h, w_in, decay_logit, gn_w, w_out, s0, latent):
    B, T, _ = h.shape
    proj = (h @ w_in).astype(F32)
    q, k, v, z = jnp.split(proj, [RET_QK, 2 * RET_QK, 2 * RET_QK + RET_V], axis=-1)
    q = q.reshape(B, T, RET_HEADS, RET_DK)
    k = k.reshape(B, T, RET_HEADS, RET_DK) * (RET_DK ** -0.5)
    v = v.reshape(B, T, RET_HEADS, RET_DV)
    if latent:
        q, k = _axial_rope(q), _axial_rope(k)
    log_gamma = jax.nn.log_sigmoid(decay_logit.astype(F32))
    s0 = s0.astype(F32)
    o_f, s_f = _retention_scan(q, k, v, log_gamma[0], s0[:, 0])
    o_b, s_b = _retention_scan(_flip(q), _flip(k), _flip(v), log_gamma[1], s0[:, 1])
    o = o_f + _flip(o_b)
    mu = jnp.mean(o, -1, keepdims=True)
    var = jnp.mean(jnp.square(o - mu), -1, keepdims=True)
    o = ((o - mu) * lax.rsqrt(var + EPS)).reshape(B, T, RET_V) * gn_w.astype(F32)
    o = o * jax.nn.silu(z)
    y = o.astype(h.dtype) @ w_out
    return y, jnp.stack([s_f, s_b], axis=1)


def _centred_dwconv(x, w):
    pad = CONV_W // 2
    return lax.conv_general_dilated(x, w[:, None, :].astype(x.dtype), window_strides=(1,),
                                    padding=[(pad, pad)], dimension_numbers=('NWC', 'WIO', 'NWC'),
                                    feature_group_count=x.shape[-1])


def _l2norm(x):
    return x * lax.rsqrt(jnp.sum(jnp.square(x), -1, keepdims=True) + EPS)


def _delta_branch(h, w_in, conv_w, a_log, dt_bias, norm_w, w_out, s0):
    B, T, _ = h.shape
    proj = (h @ w_in).astype(F32)
    qkv, z, a, b = jnp.split(proj, [DEL_CONV, DEL_CONV + DEL_V, DEL_CONV + DEL_V + 2 * DEL_HEADS], axis=-1)
    qkv = jax.nn.silu(_centred_dwconv(qkv, conv_w.astype(F32)))
    q, k, v = jnp.split(qkv, [DEL_QK, 2 * DEL_QK], axis=-1)
    q = _l2norm(q.reshape(B, T, DEL_HEADS, DEL_DK)) * (DEL_DK ** -0.5)
    k = _l2norm(k.reshape(B, T, DEL_HEADS, DEL_DK))
    v = v.reshape(B, T, DEL_HEADS, DEL_DV)
    beta = jax.nn.sigmoid(b.reshape(B, T, 2, DEL_HEADS))
    g = -jnp.exp(a_log.astype(F32)) * jax.nn.softplus(a.reshape(B, T, 2, DEL_HEADS) + dt_bias.astype(F32))
    s0 = s0.astype(F32)
    o_f, s_f = _gated_delta_scan(q, k, v, beta[:, :, 0], g[:, :, 0], s0[:, 0])
    o_b, s_b = _gated_delta_scan(_flip(q), _flip(k), _flip(v), _flip(beta[:, :, 1]), _flip(g[:, :, 1]), s0[:, 1])
    o = o_f + _flip(o_b)
    o = o * lax.rsqrt(jnp.mean(jnp.square(o), -1, keepdims=True) + EPS)
    o = o.reshape(B, T, DEL_V) * norm_w.astype(F32) * jax.nn.silu(z)
    y = o.astype(h.dtype) @ w_out
    return y, jnp.stack([s_f, s_b], axis=1)


def _trunk(x, cond, s_ret, s_del, latent, norm_w, mod_w, mod_b, ret_w_in, ret_decay, ret_gn_w, ret_w_out,
           del_w_in, del_conv_w, del_a_log, del_dt_bias, del_norm_w, del_w_out, final_norm_w):
    sc = jax.nn.silu(cond.astype(F32))
    ret_states, del_states = [], []
    for i in range(DEPTH):
        mod = sc @ mod_w[i].astype(F32) + mod_b[i].astype(F32)
        shift, scale, gate = [m[:, None, :] for m in jnp.split(mod, 3, axis=-1)]
        h = (_rmsnorm_f32(x, norm_w[i]) * (1.0 + scale) + shift).astype(x.dtype)
        j = i // N_MIXERS
        if i % N_MIXERS == 0:
            y, s = _retention_branch(h, ret_w_in[j], ret_decay[j], ret_gn_w[j], ret_w_out[j], s_ret[:, j], latent)
            ret_states.append(s)
        else:
            y, s = _delta_branch(h, del_w_in[j], del_conv_w[j], del_a_log[j], del_dt_bias[j],
                                 del_norm_w[j], del_w_out[j], s_del[:, j])
            del_states.append(s)
        x = x + (gate * y.astype(F32)).astype(x.dtype)
    y_out = _rmsnorm_f32(x, final_norm_w).astype(x.dtype)
    if latent:
        return y_out, None, None
    return y_out, jnp.stack(ret_states, axis=1), jnp.stack(del_states, axis=1)


def setup_inputs(seed: int = 0) -> dict:
    key = jax.random.key(seed)
    ks = jax.random.split(key, 24)
    nrm = jax.random.normal
    D = D_MODEL
    gam = 1.0 - 2.0 ** (-5.0 - np.arange(RET_HEADS))
    gam_logit = jnp.asarray(np.log(gam / (1.0 - gam)), dtype=F32)
    dt = jnp.exp(jax.random.uniform(ks[17], (N_DEL, 2, DEL_HEADS), F32, math.log(1e-3), math.log(1e-1)))
    return {
        "x_prompt": nrm(ks[0], (BATCH, SEQ, D), F32),
        "x_sample": nrm(ks[1], (DEC_BATCH, DEC_SEQ, D), F32),
        "state_ret": 0.5 * nrm(ks[2], (DEC_BATCH, N_RET, 2, RET_HEADS, RET_DK, RET_DV), F32),
        "state_delta": (DEL_DK ** -0.5) * nrm(ks[3], (DEC_BATCH, N_DEL, 2, DEL_HEADS, DEL_DK, DEL_DV), F32),
        "c": nrm(ks[4], (DEC_BATCH, D), F32),
        "c_ctx": nrm(ks[5], (D,), F32),
        "norm_w": 1.0 + 0.02 * nrm(ks[6], (DEPTH, D), F32),
        "mod_w": (D ** -0.5) * nrm(ks[7], (DEPTH, D, 3 * D), F32),
        "mod_b": 0.02 * nrm(ks[8], (DEPTH, 3 * D), F32),
        "ret_w_in": (D ** -0.5) * nrm(ks[9], (N_RET, D, RET_IN), F32),
        "ret_decay": gam_logit[None, None, :] + 0.1 * nrm(ks[10], (N_RET, 2, RET_HEADS), F32),
        "ret_gn_w": 1.0 + 0.02 * nrm(ks[11], (N_RET, RET_V), F32),
        "ret_w_out": (RET_V ** -0.5) * nrm(ks[12], (N_RET, RET_V, D), F32),
        "del_w_in": (D ** -0.5) * nrm(ks[13], (N_DEL, D, DEL_IN), F32),
        "del_conv_w": (CONV_W ** -0.5) * nrm(ks[14], (N_DEL, CONV_W, DEL_CONV), F32),
        "del_a_log": jnp.log(jax.random.uniform(ks[15], (N_DEL, 2, DEL_HEADS), F32, 1.0, 16.0)),
        "del_dt_bias": dt + jnp.log(-jnp.expm1(-dt)),
        "del_norm_w": 1.0 + 0.02 * nrm(ks[18], (N_DEL, DEL_V), F32),
        "del_w_out": (DEL_V ** -0.5) * nrm(ks[19], (N_DEL, DEL_V, D), F32),
        "final_norm_w": 1.0 + 0.02 * nrm(ks[20], (D,), F32),
    }


def reference(x_prompt, x_sample, state_ret, state_delta, c, c_ctx, norm_w, mod_w, mod_b, ret_w_in, ret_decay,
              ret_gn_w, ret_w_out, del_w_in, del_conv_w, del_a_log, del_dt_bias, del_norm_w, del_w_out,
              final_norm_w):
    B = x_prompt.shape[0]
    zero_ret = jnp.zeros((B, N_RET, 2, RET_HEADS, RET_DK, RET_DV), F32)
    zero_del = jnp.zeros((B, N_DEL, 2, DEL_HEADS, DEL_DK, DEL_DV), F32)
    y_prompt, new_state_ret, new_state_delta = _trunk(
        x_prompt, c_ctx[None, :], zero_ret, zero_del, False, norm_w, mod_w, mod_b, ret_w_in, ret_decay,
        ret_gn_w, ret_w_out, del_w_in, del_conv_w, del_a_log, del_dt_bias, del_norm_w, del_w_out, final_norm_w)
    y_sample, _, _ = _trunk(
        x_sample, c, state_ret, state_delta, True, norm_w, mod_w, mod_b, ret_w_in, ret_decay,
        ret_gn_w, ret_w_out, del_w_in, del_conv_w, del_a_log, del_dt_bias, del_norm_w, del_w_out, final_norm_w)
    return (y_prompt, y_sample, new_state_ret, new_state_delta)
```

```python
import functools
import math

import jax
import jax.numpy as jnp
from jax import lax
from jax.experimental import pallas as pl
from jax.experimental.pallas import tpu as pltpu

F32 = jnp.float32
BF16 = jnp.bfloat16

D_MODEL = 1024
DEPTH = 4
GRID_W = 64
CHUNK = 64
EPS = 1e-6
ROPE_BASE = 10000.0

RET_HEADS = 4
RET_DK = 256
RET_DV = 512
RET_QK = RET_HEADS * RET_DK
RET_V = RET_HEADS * RET_DV
RET_IN = 2 * RET_QK + 2 * RET_V

DEL_HEADS = 8
DEL_DK = 128
DEL_DV = 256
DEL_QK = DEL_HEADS * DEL_DK
DEL_V = DEL_HEADS * DEL_DV
DEL_CONV = 2 * DEL_QK + DEL_V
DEL_MAIN = DEL_CONV + DEL_V
DEL_AB = 4 * DEL_HEADS

ROW_BLOCK = 256
SUPER = 256
LANES = 128
VMEM_LIMIT = 56 * 1024 * 1024


def _sigmoid(x):
    return 1.0 / (1.0 + jnp.exp(-x))


def _silu(x):
    return x * _sigmoid(x)


def _softplus(x):
    return jnp.maximum(x, 0.0) + jnp.log1p(jnp.exp(-jnp.abs(x)))


def _log_sigmoid(x):
    return jnp.minimum(x, 0.0) - jnp.log1p(jnp.exp(-jnp.abs(x)))


def _dot(a, b):
    return jnp.dot(a, b, preferred_element_type=F32)


def _dot_nt(a, b):
    return lax.dot_general(a, b, (((1,), (1,)), ((), ())), preferred_element_type=F32)


def _dot_tn(a, b):
    return lax.dot_general(a, b, (((0,), (0,)), ((), ())), preferred_element_type=F32)


def _mod_body(cond_ref, w_ref, b_ref, o_ref):
    sc = _silu(cond_ref[...])
    o_ref[...] = jnp.dot(sc, w_ref[...], preferred_element_type=F32,
                         precision=lax.Precision.HIGHEST) + b_ref[...]


def _modulation(cond8, mod_w, mod_b):
    nb = 3
    return pl.pallas_call(
        _mod_body,
        grid=(DEPTH, nb),
        in_specs=[
            pl.BlockSpec((8, D_MODEL), lambda i, j: (0, 0)),
            pl.BlockSpec((None, D_MODEL, D_MODEL), lambda i, j: (i, 0, j)),
            pl.BlockSpec((None, 1, D_MODEL), lambda i, j: (i, 0, j)),
        ],
        out_specs=pl.BlockSpec((None, 8, D_MODEL), lambda i, j: (i, 0, j)),
        out_shape=jax.ShapeDtypeStruct((DEPTH, 8, 3 * D_MODEL), F32),
        compiler_params=pltpu.CompilerParams(vmem_limit_bytes=VMEM_LIMIT),
        name="modulation",
    )(cond8, mod_w, mod_b.reshape(DEPTH, 1, 3 * D_MODEL))


def _inproj_body(cidx_ref, x_ref, mod_ref, nw_ref, w_ref, *rest, has_ab):
    if has_ab:
        wab_ref, o_ref, ab_ref = rest
    else:
        (o_ref,) = rest
    ci = cidx_ref[pl.program_id(0)]
    x = x_ref[...]
    ms = jnp.mean(x * x, axis=-1, keepdims=True)
    xn = x * lax.rsqrt(ms + EPS) * nw_ref[...]
    m = mod_ref[pl.ds(ci, 1), :]
    shift = m[:, :D_MODEL]
    scale = m[:, D_MODEL:2 * D_MODEL]
    h = (xn * (1.0 + scale) + shift).astype(BF16)
    o_ref[...] = _dot(h, w_ref[...])
    if has_ab:
        ab_ref[...] = _dot(h, wab_ref[...])


def _inproj(x, cidx, mod, nw, w, wab=None):
    n = x.shape[0]
    ncols = w.shape[1]
    has_ab = wab is not None
    in_specs = [
        pl.BlockSpec((ROW_BLOCK, D_MODEL), lambda i, c: (i, 0)),
        pl.BlockSpec((8, 3 * D_MODEL), lambda i, c: (0, 0)),
        pl.BlockSpec((1, D_MODEL), lambda i, c: (0, 0)),
        pl.BlockSpec((D_MODEL, ncols), lambda i, c: (0, 0)),
    ]
    out_specs = [pl.BlockSpec((ROW_BLOCK, ncols), lambda i, c: (i, 0))]
    out_shape = [jax.ShapeDtypeStruct((n, ncols), F32)]
    args = [x, mod, nw, w]
    if has_ab:
        in_specs.append(pl.BlockSpec((D_MODEL, LANES), lambda i, c: (0, 0)))
        out_specs.append(pl.BlockSpec((ROW_BLOCK, LANES), lambda i, c: (i, 0)))
        out_shape.append(jax.ShapeDtypeStruct((n, LANES), F32))
        args.append(wab)
    res = pl.pallas_call(
        functools.partial(_inproj_body, has_ab=has_ab),
        grid_spec=pltpu.PrefetchScalarGridSpec(
            num_scalar_prefetch=1, grid=(n // ROW_BLOCK,), in_specs=in_specs, out_specs=out_specs),
        out_shape=out_shape,
        compiler_params=pltpu.CompilerParams(vmem_limit_bytes=VMEM_LIMIT),
        name="inproj",
    )(cidx, *args)
    return res if has_ab else res[0]


def _outproj_body(cidx_ref, x_ref, o_ref, mod_ref, w_ref, *rest, final):
    if final:
        fnw_ref, out_ref = rest
    else:
        (out_ref,) = rest
    ci = cidx_ref[pl.program_id(0)]
    gate = mod_ref[pl.ds(ci, 1), :][:, 2 * D_MODEL:]
    xn = x_ref[...] + gate * _dot(o_ref[...], w_ref[...])
    if final:
        ms = jnp.mean(xn * xn, axis=-1, keepdims=True)
        xn = xn * lax.rsqrt(ms + EPS) * fnw_ref[...]
    out_ref[...] = xn


def _outproj(x, o, cidx, mod, w, fnw=None):
    n = x.shape[0]
    final = fnw is not None
    vdim = o.shape[1]
    in_specs = [
        pl.BlockSpec((ROW_BLOCK, D_MODEL), lambda i, c: (i, 0)),
        pl.BlockSpec((ROW_BLOCK, vdim), lambda i, c: (i, 0)),
        pl.BlockSpec((8, 3 * D_MODEL), lambda i, c: (0, 0)),
        pl.BlockSpec((vdim, D_MODEL), lambda i, c: (0, 0)),
    ]
    args = [x, o, mod, w]
    if final:
        in_specs.append(pl.BlockSpec((1, D_MODEL), lambda i, c: (0, 0)))
        args.append(fnw)
    return pl.pallas_call(
        functools.partial(_outproj_body, final=final),
        grid_spec=pltpu.PrefetchScalarGridSpec(
            num_scalar_prefetch=1, grid=(n // ROW_BLOCK,), in_specs=in_specs,
            out_specs=pl.BlockSpec((ROW_BLOCK, D_MODEL), lambda i, c: (i, 0))),
        out_shape=jax.ShapeDtypeStruct((n, D_MODEL), F32),
        compiler_params=pltpu.CompilerParams(vmem_limit_bytes=VMEM_LIMIT),
        name="outproj",
    )(cidx, *args)


def _ret_body(decay_ref, q_ref, k_ref, v_ref, z_ref, gnw_ref, *rest, T, latent):
    if latent:
        cos_ref, sin_ref, s0_ref, o_ref = rest
    else:
        o_ref, st_ref = rest
    hd = pl.program_id(1)
    lgf = _log_sigmoid(jnp.full((1, 1), decay_ref[0, hd], F32))
    lgb = _log_sigmoid(jnp.full((1, 1), decay_ref[1, hd], F32))
    scale = RET_DK ** -0.5

    q = q_ref[...]
    k = k_ref[...]
    if latent:
        cos = cos_ref[...]
        sin = sin_ref[...]
        hk = RET_DK // 2

        def rope(x):
            x1, x2 = x[:, :hk], x[:, hk:]
            return jnp.concatenate([x1 * cos - x2 * sin, x1 * sin + x2 * cos], axis=-1)

        q = rope(q)
        k = rope(k)
    kb = k.astype(BF16)
    vb = v_ref[...].astype(BF16)
    tcol = lax.broadcasted_iota(jnp.int32, (T, 1), 0).astype(F32)

    if latent:
        s0f = s0_ref[0].astype(BF16)
        s0b = s0_ref[1].astype(BF16)

    gnw = gnw_ref[...]
    QT = 256
    for qi in range(T // QT):
        rows = slice(qi * QT, (qi + 1) * QT)
        qt = q[rows]
        ti = lax.broadcasted_iota(jnp.int32, (QT, T), 0) + qi * QT
        tj = lax.broadcasted_iota(jnp.int32, (QT, T), 1)
        dd = (ti - tj).astype(F32)
        lg = jnp.where(dd > 0, lgf, lgb)
        mask = jnp.where(dd == 0, 2.0 * scale, jnp.exp(lg * jnp.abs(dd)) * scale)
        s = _dot_nt(qt.astype(BF16), kb) * mask
        o = _dot(s.astype(BF16), vb)
        if latent:
            tq = tcol[rows]
            xif = jnp.exp(lgf * (tq + 1.0))
            xib = jnp.exp(lgb * (float(T) - tq))
            o = o + _dot((qt * xif).astype(BF16), s0f) + _dot((qt * xib).astype(BF16), s0b)
        mu = jnp.mean(o, axis=-1, keepdims=True)
        oc = o - mu
        var = jnp.mean(oc * oc, axis=-1, keepdims=True)
        on = oc * lax.rsqrt(var + EPS) * gnw
        o_ref[rows, :] = (on * _silu(z_ref[rows, :])).astype(o_ref.dtype)

    if not latent:
        zf = jnp.exp(lgf * (float(T - 1) - tcol)) * scale
        zb = jnp.exp(lgb * tcol) * scale
        st_ref[0] = _dot_tn((k * zf).astype(BF16), vb)
        st_ref[1] = _dot_tn((k * zb).astype(BF16), vb)


def _ret_mix(proj, decay, gnw, T, *, j, rope=None, s0=None, state_in=None):
    n = proj.shape[0]
    nseq = n // T
    latent = rope is not None
    smem = pl.BlockSpec(memory_space=pltpu.SMEM)
    in_specs = [
        smem,
        pl.BlockSpec((T, RET_DK), lambda b, h: (b, h)),
        pl.BlockSpec((T, RET_DK), lambda b, h: (b, RET_HEADS + h)),
        pl.BlockSpec((T, RET_DV), lambda b, h: (b, RET_HEADS + h)),
        pl.BlockSpec((T, RET_DV), lambda b, h: (b, 2 * RET_HEADS + h)),
        pl.BlockSpec((1, RET_DV), lambda b, h: (0, h)),
    ]
    args = [decay, proj, proj, proj, proj, gnw]
    o_spec = pl.BlockSpec((T, RET_DV), lambda b, h: (b, h))
    o_shape = jax.ShapeDtypeStruct((n, RET_V), BF16)
    aliases = {}
    if latent:
        cos, sin = rope
        in_specs += [
            pl.BlockSpec((T, LANES), lambda b, h: (0, 0)),
            pl.BlockSpec((T, LANES), lambda b, h: (0, 0)),
            pl.BlockSpec((None, None, 2, None, RET_DK, RET_DV), lambda b, h: (b, j, 0, h, 0, 0)),
        ]
        args += [cos, sin, s0]
        out_specs = o_spec
        out_shape = o_shape
    else:
        st_spec = pl.BlockSpec((None, None, 2, None, RET_DK, RET_DV), lambda b, h: (b, j, 0, h, 0, 0))
        st_shape = jax.ShapeDtypeStruct((nseq, 2, 2, RET_HEADS, RET_DK, RET_DV), F32)
        out_specs = [o_spec, st_spec]
        out_shape = [o_shape, st_shape]
        if state_in is not None:
            in_specs.append(pl.BlockSpec(memory_space=pl.ANY))
            args.append(state_in)
            aliases = {len(args) - 1: 1}
    body = functools.partial(_ret_body, T=T, latent=latent)
    if state_in is not None:
        inner = body

        def body(*refs):
            n_in = len(args)
            return inner(*refs[:n_in - 1], *refs[n_in:])

    return pl.pallas_call(
        body,
        grid=(nseq, RET_HEADS),
        in_specs=in_specs,
        out_specs=out_specs,
        out_shape=out_shape,
        input_output_aliases=aliases,
        compiler_params=pltpu.CompilerParams(vmem_limit_bytes=VMEM_LIMIT),
        name="ret_mix_lat" if latent else "ret_mix_ctx",
    )(*args)


def _seg_cumsum(x, pos, axis, reverse):
    n = x.shape[axis]
    s = 1
    while s < CHUNK:
        if reverse:
            x = x + jnp.where(pos < CHUNK - s, pltpu.roll(x, n - s, axis), 0.0)
        else:
            x = x + jnp.where(pos >= s, pltpu.roll(x, s, axis), 0.0)
        s *= 2
    return x


def _delta_body(alog_ref, dtb_ref, q_ref, k_ref, v_ref, z_ref, cwq_ref, cwk_ref, cwv_ref, ab_ref, abt_ref,
                nw_ref, *rest, T, latent):
    if latent:
        s0_ref, o_ref = rest[:2]
        scratch = rest[2:]
    else:
        o_ref, st_ref = rest[:2]
        scratch = rest[2:]
    u_s, w_s, qe_s, ke_s, qkl_s, et_s, oacc_s = scratch
    hd = pl.program_id(1)
    nch = T // CHUNK

    trow = lax.broadcasted_iota(jnp.int32, (T, 1), 0)

    def conv(x, w):
        xp = jnp.where(trow >= 1, pltpu.roll(x, 1, 0), 0.0)
        xn = jnp.where(trow <= T - 2, pltpu.roll(x, T - 1, 0), 0.0)
        return _silu(w[0:1, :] * xp + w[1:2, :] * x + w[2:3, :] * xn)

    q = conv(q_ref[...], cwq_ref[...])
    k = conv(k_ref[...], cwk_ref[...])
    v = conv(v_ref[...], cwv_ref[...])
    q = q * lax.rsqrt(jnp.sum(q * q, axis=-1, keepdims=True) + EPS) * (DEL_DK ** -0.5)
    k = k * lax.rsqrt(jnp.sum(k * k, axis=-1, keepdims=True) + EPS)
    qb = q.astype(BF16)
    kb = k.astype(BF16)

    ab = ab_ref[...]
    lane = lax.broadcasted_iota(jnp.int32, (T, LANES), 1)
    pos_c = trow & (CHUNK - 1)
    pos_r = lax.broadcasted_iota(jnp.int32, (1, T), 1) & (CHUNK - 1)

    ii = lax.broadcasted_iota(jnp.int32, (SUPER, SUPER), 0)
    jj = lax.broadcasted_iota(jnp.int32, (SUPER, SUPER), 1)
    same = (ii >> 6) == (jj >> 6)
    eye = (ii == jj).astype(F32)
    xor = ii ^ jj
    lvl = jnp.zeros((SUPER, SUPER), jnp.int32)
    for p in range(1, CHUNK.bit_length() - 1):
        lvl = lvl + (xor >= (1 << p)).astype(jnp.int32)

    per_dir = []
    for d in range(2):
        col = d * DEL_HEADS + hd
        a_col = jnp.sum(jnp.where(lane == col, ab, 0.0), axis=1, keepdims=True)
        b_col = jnp.sum(jnp.where(lane == 2 * DEL_HEADS + col, ab, 0.0), axis=1, keepdims=True)
        neg_a = -jnp.exp(jnp.full((1, 1), alog_ref[d, hd], F32))
        dtb = dtb_ref[d, hd]
        g_col = neg_a * _softplus(a_col + dtb)
        beta = _sigmoid(b_col)
        g_row = neg_a * _softplus(abt_ref[pl.ds(col, 1), :] + dtb)
        pre = _seg_cumsum(g_col, pos_c, 0, False)
        suf = _seg_cumsum(g_col, pos_c, 0, True)
        G = suf if d else pre
        tot = pre + suf - g_col
        G_row = _seg_cumsum(g_row, pos_r, 1, bool(d))
        per_dir.append((beta, G, tot, G_row))

    for r in range(T // SUPER):
        rows = slice(r * SUPER, (r + 1) * SUPER)
        kk = _dot_nt(kb[rows], kb[rows])
        qk = _dot_nt(qb[rows], kb[rows])
        for d in range(2):
            beta, G, tot, G_row = per_dir[d]
            bt = beta[rows]
            Gc = G[rows]
            incl = same & ((ii <= jj) if d else (ii >= jj))
            strict = same & ((ii < jj) if d else (ii > jj))
            gdiff = Gc - G_row[:, rows]
            L = jnp.where(incl, jnp.exp(jnp.where(incl, gdiff, 0.0)), 0.0)
            A = jnp.where(strict, bt * kk * L, 0.0)
            X = eye - jnp.where(lvl == 0, A, 0.0)
            for p in range(1, CHUNK.bit_length() - 1):
                Xb = X.astype(BF16)
                P = _dot(Xb, jnp.where(lvl == p, A, 0.0).astype(BF16))
                X = X - _dot(P.astype(BF16), Xb)
            Xb = X.astype(BF16)
            eG = jnp.exp(Gc)
            u_s[d, rows, :] = _dot(Xb, (v[rows] * bt).astype(BF16))
            w_s[d, rows, :] = _dot(Xb, (k[rows] * (bt * eG)).astype(BF16)).astype(BF16)
            qe_s[d, rows, :] = (q[rows] * eG).astype(BF16)
            ke_s[d, rows, :] = (k[rows] * jnp.exp(tot[rows] - Gc)).astype(BF16)
            et_s[d, rows, :] = jnp.exp(tot[rows])
            qkl = qk * L
            for c in range(SUPER // CHUNK):
                cs = slice(c * CHUNK, (c + 1) * CHUNK)
                qkl_s[d, r * SUPER + c * CHUNK:r * SUPER + (c + 1) * CHUNK, :] = qkl[cs, cs].astype(BF16)

    for d in range(2):
        if latent:
            S0 = s0_ref[d]
        else:
            S0 = jnp.zeros((DEL_DK, DEL_DV), F32)

        def step(i, S, d=d):
            c = (nch - 1 - i) if d else i
            r0 = pl.multiple_of(c * CHUNK, CHUNK)
            rows = pl.ds(r0, CHUNK)
            Sb = S.astype(BF16)
            v_new = u_s[d, rows, :] - _dot(w_s[d, rows, :], Sb)
            vnb = v_new.astype(BF16)
            o = _dot(qe_s[d, rows, :], Sb) + _dot(qkl_s[d, rows, :], vnb)
            if d:
                oacc_s[rows, :] = oacc_s[rows, :] + o
            else:
                oacc_s[rows, :] = o
            et = et_s[d, pl.ds(r0, 1), :]
            return S * et + _dot_tn(ke_s[d, rows, :], vnb)

        S_fin = lax.fori_loop(0, nch, step, S0)
        if not latent:
            st_ref[d] = S_fin

    o = oacc_s[...]
    o = o * lax.rsqrt(jnp.mean(o * o, axis=-1, keepdims=True) + EPS)
    o_ref[...] = (o * nw_ref[...] * _silu(z_ref[...])).astype(o_ref.dtype)


def _delta_mix(proj, ab, abt, alog, dtb, convw, nw, T, *, j, s0=None, state_in=None):
    n = proj.shape[0]
    nseq = n // T
    latent = s0 is not None
    H = DEL_HEADS
    smem = pl.BlockSpec(memory_space=pltpu.SMEM)
    in_specs = [
        smem, smem,
        pl.BlockSpec((T, DEL_DK), lambda b, h: (b, h)),
        pl.BlockSpec((T, DEL_DK), lambda b, h: (b, H + h)),
        pl.BlockSpec((T, DEL_DV), lambda b, h: (b, H + h)),
        pl.BlockSpec((T, DEL_DV), lambda b, h: (b, 2 * H + h)),
        pl.BlockSpec((3, DEL_DK), lambda b, h: (0, h)),
        pl.BlockSpec((3, DEL_DK), lambda b, h: (0, H + h)),
        pl.BlockSpec((3, DEL_DV), lambda b, h: (0, H + h)),
        pl.BlockSpec((T, LANES), lambda b, h: (b, 0)),
        pl.BlockSpec((DEL_AB, T), lambda b, h: (0, b)),
        pl.BlockSpec((1, DEL_DV), lambda b, h: (0, h)),
    ]
    args = [alog, dtb, proj, proj, proj, proj, convw, convw, convw, ab, abt, nw]
    o_spec = pl.BlockSpec((T, DEL_DV), lambda b, h: (b, h))
    o_shape = jax.ShapeDtypeStruct((n, DEL_V), BF16)
    aliases = {}
    if latent:
        in_specs.append(pl.BlockSpec((None, None, 2, None, DEL_DK, DEL_DV), lambda b, h: (b, j, 0, h, 0, 0)))
        args.append(s0)
        out_specs = o_spec
        out_shape = o_shape
    else:
        st_spec = pl.BlockSpec((None, None, 2, None, DEL_DK, DEL_DV), lambda b, h: (b, j, 0, h, 0, 0))
        st_shape = jax.ShapeDtypeStruct((nseq, 2, 2, H, DEL_DK, DEL_DV), F32)
        out_specs = [o_spec, st_spec]
        out_shape = [o_shape, st_shape]
        if state_in is not None:
            in_specs.append(pl.BlockSpec(memory_space=pl.ANY))
            args.append(state_in)
            aliases = {len(args) - 1: 1}
    body = functools.partial(_delta_body, T=T, latent=latent)
    if state_in is not None:
        inner = body

        def body(*refs):
            n_in = len(args)
            return inner(*refs[:n_in - 1], *refs[n_in:])

    scratch = [
        pltpu.VMEM((2, T, DEL_DV), F32),
        pltpu.VMEM((2, T, DEL_DK), BF16),
        pltpu.VMEM((2, T, DEL_DK), BF16),
        pltpu.VMEM((2, T, DEL_DK), BF16),
        pltpu.VMEM((2, T, CHUNK), BF16),
        pltpu.VMEM((2, T, 1), F32),
        pltpu.VMEM((T, DEL_DV), F32),
    ]
    return pl.pallas_call(
        body,
        grid=(nseq, H),
        in_specs=in_specs,
        out_specs=out_specs,
        out_shape=out_shape,
        scratch_shapes=scratch,
        input_output_aliases=aliases,
        compiler_params=pltpu.CompilerParams(vmem_limit_bytes=VMEM_LIMIT),
        name="delta_mix_lat" if latent else "delta_mix_ctx",
    )(*args)


def _rope_tables(T):
    rows = T // GRID_W
    r = jnp.broadcast_to(jnp.arange(rows)[:, None], (rows, GRID_W)).reshape(T).astype(F32)
    col = jnp.broadcast_to(jnp.arange(GRID_W)[None, :], (rows, GRID_W)).reshape(T).astype(F32)
    n_pairs = RET_DK // 4
    freqs = ROPE_BASE ** (-jnp.arange(n_pairs, dtype=F32) / n_pairs)
    ang = jnp.concatenate([r[:, None] * freqs, col[:, None] * freqs], -1)
    return jnp.cos(ang), jnp.sin(ang)


def kernel(x_prompt, x_sample, state_ret, state_delta, c, c_ctx, norm_w, mod_w, mod_b, ret_w_in, ret_decay,
           ret_gn_w, ret_w_out, del_w_in, del_conv_w, del_a_log, del_dt_bias, del_norm_w, del_w_out,
           final_norm_w):
    B, T_ctx, _ = x_prompt.shape
    Bd, T_lat, _ = x_sample.shape
    n_ctx = B * T_ctx
    n_lat = Bd * T_lat

    x_ctx = x_prompt.reshape(n_ctx, D_MODEL)
    x_lat = x_sample.reshape(n_lat, D_MODEL)

    cond8 = jnp.zeros((8, D_MODEL), F32).at[0].set(c_ctx).at[1:1 + Bd].set(c)
    mods = _modulation(cond8, mod_w, mod_b)
    cidx_ctx = jnp.zeros((n_ctx // ROW_BLOCK,), jnp.int32)
    cidx_lat = 1 + jnp.arange(n_lat // ROW_BLOCK, dtype=jnp.int32) // (T_lat // ROW_BLOCK)

    rope = _rope_tables(T_lat)
    st_ret = None
    st_del = None
    for i in range(DEPTH):
        j = i // 2
        mod = mods[i]
        nw = norm_w[i][None, :]
        last = i == DEPTH - 1
        fnw = final_norm_w[None, :] if last else None
        if i % 2 == 0:
            w_in = ret_w_in[j].astype(BF16)
            w_out = ret_w_out[j].astype(BF16)
            gnw = ret_gn_w[j][None, :]
            p_ctx = _inproj(x_ctx, cidx_ctx, mod, nw, w_in)
            p_lat = _inproj(x_lat, cidx_lat, mod, nw, w_in)
            o_ctx, st_ret = _ret_mix(p_ctx, ret_decay[j], gnw, T_ctx, j=j, state_in=st_ret)
            o_lat = _ret_mix(p_lat, ret_decay[j], gnw, T_lat, j=j, rope=rope, s0=state_ret)
        else:
            w_full = del_w_in[j]
            w_in = w_full[:, :DEL_MAIN].astype(BF16)
            w_ab = jnp.pad(w_full[:, DEL_MAIN:], ((0, 0), (0, LANES - DEL_AB))).astype(BF16)
            w_out = del_w_out[j].astype(BF16)
            dnw = del_norm_w[j][None, :]
            p_ctx, ab_ctx = _inproj(x_ctx, cidx_ctx, mod, nw, w_in, w_ab)
            p_lat, ab_lat = _inproj(x_lat, cidx_lat, mod, nw, w_in, w_ab)
            abt_ctx = ab_ctx[:, :DEL_AB].T
            abt_lat = ab_lat[:, :DEL_AB].T
            o_ctx, st_del = _delta_mix(p_ctx, ab_ctx, abt_ctx, del_a_log[j], del_dt_bias[j], del_conv_w[j], dnw,
                                       T_ctx, j=j, state_in=st_del)
            o_lat = _delta_mix(p_lat, ab_lat, abt_lat, del_a_log[j], del_dt_bias[j], del_conv_w[j], dnw,
                               T_lat, j=j, s0=state_delta)
        x_ctx = _outproj(x_ctx, o_ctx, cidx_ctx, mod, w_out, fnw)
        x_lat = _outproj(x_lat, o_lat, cidx_lat, mod, w_out, fnw)

    y_prompt = x_ctx.reshape(B, T_ctx, D_MODEL)
    y_sample = x_lat.reshape(Bd, T_lat, D_MODEL)
    return (y_prompt, y_sample, st_ret, st_del)
```

```python
import functools

import jax
import jax.numpy as jnp
from jax import lax
from jax.experimental import pallas as pl
from jax.experimental.pallas import tpu as pltpu

F32 = jnp.float32
BF16 = jnp.bfloat16

D_MODEL = 1024
DEPTH = 4
GRID_W = 64
EPS = 1e-6
ROPE_BASE = 10000.0

RET_HEADS = 4
RET_DK = 256
RET_DV = 512
RET_QK = RET_HEADS * RET_DK
RET_V = RET_HEADS * RET_DV
RET_IN = 2 * RET_QK + 2 * RET_V

DEL_HEADS = 8
DEL_DK = 128
DEL_DV = 256
DEL_QK = DEL_HEADS * DEL_DK
DEL_V = DEL_HEADS * DEL_DV
DEL_CONV = 2 * DEL_QK + DEL_V
DEL_MAIN = DEL_CONV + DEL_V
DEL_AB = 4 * DEL_HEADS

ROW_BLOCK = 256
SUPER = 256
N_LEVELS = SUPER.bit_length() - 1
LANES = 128
BF16_ROWS = 16
VMEM_LIMIT = 56 * 1024 * 1024


def _sigmoid(x):
    return 1.0 / (1.0 + jnp.exp(-x))


def _silu(x):
    return x * _sigmoid(x)


def _softplus(x):
    return jnp.maximum(x, 0.0) + jnp.log1p(jnp.exp(-jnp.abs(x)))


def _log_sigmoid(x):
    return jnp.minimum(x, 0.0) - jnp.log1p(jnp.exp(-jnp.abs(x)))


def _dot(a, b):
    return jnp.dot(a, b, preferred_element_type=F32)


def _dot_nt(a, b):
    return lax.dot_general(a, b, (((1,), (1,)), ((), ())), preferred_element_type=F32)


def _dot_tn(a, b):
    return lax.dot_general(a, b, (((0,), (0,)), ((), ())), preferred_element_type=F32)


def _mod_body(cond_ref, w_ref, b_ref, o_ref):
    sc = _silu(cond_ref[...])
    o_ref[...] = jnp.dot(sc, w_ref[...], preferred_element_type=F32,
                         precision=lax.Precision.HIGHEST) + b_ref[...]


def _modulation(cond8, mod_w, mod_b):
    nb = 3
    return pl.pallas_call(
        _mod_body,
        grid=(DEPTH, nb),
        in_specs=[
            pl.BlockSpec((8, D_MODEL), lambda i, j: (0, 0)),
            pl.BlockSpec((None, D_MODEL, D_MODEL), lambda i, j: (i, 0, j)),
            pl.BlockSpec((None, 1, D_MODEL), lambda i, j: (i, 0, j)),
        ],
        out_specs=pl.BlockSpec((None, 8, D_MODEL), lambda i, j: (i, 0, j)),
        out_shape=jax.ShapeDtypeStruct((DEPTH, 8, 3 * D_MODEL), F32),
        compiler_params=pltpu.CompilerParams(vmem_limit_bytes=VMEM_LIMIT),
        name="modulation",
    )(cond8, mod_w, mod_b.reshape(DEPTH, 1, 3 * D_MODEL))


def _inproj_body(cidx_ref, x_ref, mod_ref, nw_ref, w_ref, *rest, has_ab):
    if has_ab:
        wab_ref, o_ref, ab_ref = rest
    else:
        (o_ref,) = rest
    ci = cidx_ref[pl.program_id(0)]
    x = x_ref[...]
    ms = jnp.mean(x * x, axis=-1, keepdims=True)
    xn = x * lax.rsqrt(ms + EPS) * nw_ref[...]
    m = mod_ref[pl.ds(ci, 1), :]
    shift = m[:, :D_MODEL]
    scale = m[:, D_MODEL:2 * D_MODEL]
    h = (xn * (1.0 + scale) + shift).astype(BF16)
    o_ref[...] = _dot(h, w_ref[...])
    if has_ab:
        ab_ref[...] = _dot(h, wab_ref[...])


def _inproj(x, cidx, mod, nw, w, wab=None):
    n = x.shape[0]
    ncols = w.shape[1]
    has_ab = wab is not None
    in_specs = [
        pl.BlockSpec((ROW_BLOCK, D_MODEL), lambda i, c: (i, 0)),
        pl.BlockSpec((8, 3 * D_MODEL), lambda i, c: (0, 0)),
        pl.BlockSpec((1, D_MODEL), lambda i, c: (0, 0)),
        pl.BlockSpec((D_MODEL, ncols), lambda i, c: (0, 0)),
    ]
    out_specs = [pl.BlockSpec((ROW_BLOCK, ncols), lambda i, c: (i, 0))]
    out_shape = [jax.ShapeDtypeStruct((n, ncols), F32)]
    args = [x, mod, nw, w]
    if has_ab:
        in_specs.append(pl.BlockSpec((D_MODEL, LANES), lambda i, c: (0, 0)))
        out_specs.append(pl.BlockSpec((ROW_BLOCK, LANES), lambda i, c: (i, 0)))
        out_shape.append(jax.ShapeDtypeStruct((n, LANES), F32))
        args.append(wab)
    res = pl.pallas_call(
        functools.partial(_inproj_body, has_ab=has_ab),
        grid_spec=pltpu.PrefetchScalarGridSpec(
            num_scalar_prefetch=1, grid=(n // ROW_BLOCK,), in_specs=in_specs, out_specs=out_specs),
        out_shape=out_shape,
        compiler_params=pltpu.CompilerParams(vmem_limit_bytes=VMEM_LIMIT),
        name="inproj",
    )(cidx, *args)
    return res if has_ab else res[0]


def _outproj_body(cidx_ref, x_ref, o_ref, mod_ref, w_ref, *rest, final):
    if final:
        fnw_ref, out_ref = rest
    else:
        (out_ref,) = rest
    ci = cidx_ref[pl.program_id(0)]
    gate = mod_ref[pl.ds(ci, 1), :][:, 2 * D_MODEL:]
    xn = x_ref[...] + gate * _dot(o_ref[...], w_ref[...])
    if final:
        ms = jnp.mean(xn * xn, axis=-1, keepdims=True)
        xn = xn * lax.rsqrt(ms + EPS) * fnw_ref[...]
    out_ref[...] = xn


def _outproj(x, o, cidx, mod, w, fnw=None):
    n = x.shape[0]
    final = fnw is not None
    vdim = o.shape[1]
    in_specs = [
        pl.BlockSpec((ROW_BLOCK, D_MODEL), lambda i, c: (i, 0)),
        pl.BlockSpec((ROW_BLOCK, vdim), lambda i, c: (i, 0)),
        pl.BlockSpec((8, 3 * D_MODEL), lambda i, c: (0, 0)),
        pl.BlockSpec((vdim, D_MODEL), lambda i, c: (0, 0)),
    ]
    args = [x, o, mod, w]
    if final:
        in_specs.append(pl.BlockSpec((1, D_MODEL), lambda i, c: (0, 0)))
        args.append(fnw)
    return pl.pallas_call(
        functools.partial(_outproj_body, final=final),
        grid_spec=pltpu.PrefetchScalarGridSpec(
            num_scalar_prefetch=1, grid=(n // ROW_BLOCK,), in_specs=in_specs,
            out_specs=pl.BlockSpec((ROW_BLOCK, D_MODEL), lambda i, c: (i, 0))),
        out_shape=jax.ShapeDtypeStruct((n, D_MODEL), F32),
        compiler_params=pltpu.CompilerParams(vmem_limit_bytes=VMEM_LIMIT),
        name="outproj",
    )(cidx, *args)


def _ret_body(decay_ref, q_ref, k_ref, v_ref, z_ref, gnw_ref, *rest, T, latent):
    if latent:
        cos_ref, sin_ref, s0_ref, o_ref = rest
    else:
        o_ref, st_ref = rest
    hd = pl.program_id(1)
    lgf = _log_sigmoid(jnp.full((1, 1), decay_ref[0, hd], F32))
    lgb = _log_sigmoid(jnp.full((1, 1), decay_ref[1, hd], F32))
    scale = RET_DK ** -0.5

    q = q_ref[...]
    k = k_ref[...]
    if latent:
        cos = cos_ref[...]
        sin = sin_ref[...]
        hk = RET_DK // 2

        def rope(x):
            x1, x2 = x[:, :hk], x[:, hk:]
            return jnp.concatenate([x1 * cos - x2 * sin, x1 * sin + x2 * cos], axis=-1)

        q = rope(q)
        k = rope(k)
    kb = k.astype(BF16)
    vb = v_ref[...].astype(BF16)
    tcol = lax.broadcasted_iota(jnp.int32, (T, 1), 0).astype(F32)

    if latent:
        s0f = s0_ref[0].astype(BF16)
        s0b = s0_ref[1].astype(BF16)

    gnw = gnw_ref[...]
    QT = 256
    for qi in range(T // QT):
        rows = slice(qi * QT, (qi + 1) * QT)
        qt = q[rows]
        ti = lax.broadcasted_iota(jnp.int32, (QT, T), 0) + qi * QT
        tj = lax.broadcasted_iota(jnp.int32, (QT, T), 1)
        dd = (ti - tj).astype(F32)
        lg = jnp.where(dd > 0, lgf, lgb)
        mask = jnp.where(dd == 0, 2.0 * scale, jnp.exp(lg * jnp.abs(dd)) * scale)
        s = _dot_nt(qt.astype(BF16), kb) * mask
        o = _dot(s.astype(BF16), vb)
        if latent:
            tq = tcol[rows]
            xif = jnp.exp(lgf * (tq + 1.0))
            xib = jnp.exp(lgb * (float(T) - tq))
            o = o + _dot((qt * xif).astype(BF16), s0f) + _dot((qt * xib).astype(BF16), s0b)
        mu = jnp.mean(o, axis=-1, keepdims=True)
        oc = o - mu
        var = jnp.mean(oc * oc, axis=-1, keepdims=True)
        on = oc * lax.rsqrt(var + EPS) * gnw
        o_ref[rows, :] = (on * _silu(z_ref[rows, :])).astype(o_ref.dtype)

    if not latent:
        zf = jnp.exp(lgf * (float(T - 1) - tcol)) * scale
        zb = jnp.exp(lgb * tcol) * scale
        st_ref[0] = _dot_tn((k * zf).astype(BF16), vb)
        st_ref[1] = _dot_tn((k * zb).astype(BF16), vb)


def _ret_mix(proj, decay, gnw, T, *, j, rope=None, s0=None, state_in=None):
    n = proj.shape[0]
    nseq = n // T
    latent = rope is not None
    smem = pl.BlockSpec(memory_space=pltpu.SMEM)
    in_specs = [
        smem,
        pl.BlockSpec((T, RET_DK), lambda b, h: (b, h)),
        pl.BlockSpec((T, RET_DK), lambda b, h: (b, RET_HEADS + h)),
        pl.BlockSpec((T, RET_DV), lambda b, h: (b, RET_HEADS + h)),
        pl.BlockSpec((T, RET_DV), lambda b, h: (b, 2 * RET_HEADS + h)),
        pl.BlockSpec((1, RET_DV), lambda b, h: (0, h)),
    ]
    args = [decay, proj, proj, proj, proj, gnw]
    o_spec = pl.BlockSpec((T, RET_DV), lambda b, h: (b, h))
    o_shape = jax.ShapeDtypeStruct((n, RET_V), BF16)
    aliases = {}
    if latent:
        cos, sin = rope
        in_specs += [
            pl.BlockSpec((T, LANES), lambda b, h: (0, 0)),
            pl.BlockSpec((T, LANES), lambda b, h: (0, 0)),
            pl.BlockSpec((None, None, 2, None, RET_DK, RET_DV), lambda b, h: (b, j, 0, h, 0, 0)),
        ]
        args += [cos, sin, s0]
        out_specs = o_spec
        out_shape = o_shape
    else:
        st_spec = pl.BlockSpec((None, None, 2, None, RET_DK, RET_DV), lambda b, h: (b, j, 0, h, 0, 0))
        st_shape = jax.ShapeDtypeStruct((nseq, 2, 2, RET_HEADS, RET_DK, RET_DV), F32)
        out_specs = [o_spec, st_spec]
        out_shape = [o_shape, st_shape]
        if state_in is not None:
            in_specs.append(pl.BlockSpec(memory_space=pl.ANY))
            args.append(state_in)
            aliases = {len(args) - 1: 1}
    body = functools.partial(_ret_body, T=T, latent=latent)
    if state_in is not None:
        inner = body

        def body(*refs):
            n_in = len(args)
            return inner(*refs[:n_in - 1], *refs[n_in:])

    return pl.pallas_call(
        body,
        grid=(nseq, RET_HEADS),
        in_specs=in_specs,
        out_specs=out_specs,
        out_shape=out_shape,
        input_output_aliases=aliases,
        compiler_params=pltpu.CompilerParams(vmem_limit_bytes=VMEM_LIMIT),
        name="ret_mix_lat" if latent else "ret_mix_ctx",
    )(*args)


def _seg_cumsum(x, pos, axis, reverse):
    n = x.shape[axis]
    s = 1
    while s < SUPER:
        if reverse:
            x = x + jnp.where(pos < SUPER - s, pltpu.roll(x, n - s, axis), 0.0)
        else:
            x = x + jnp.where(pos >= s, pltpu.roll(x, s, axis), 0.0)
        s *= 2
    return x


def _sibling_rows(x, b, sib):
    return jnp.concatenate([x[(2 * m + sib) * b:(2 * m + sib + 1) * b] for m in range(x.shape[0] // (2 * b))], axis=0)


def _tri_inverses(Abs, dirs, eye_s, lm_s, lmh_s):
    n = len(Abs)
    Xs = [eye_s[...] - Abs[c] * lm_s[dirs[c], 0] for c in range(n)]
    for p in range(1, N_LEVELS):
        b = 1 << p
        if b < BF16_ROWS:
            Ps = [_dot(Xs[c], Abs[c]) for c in range(n)]
            Pm = [Ps[c].astype(BF16) * lm_s[dirs[c], p] for c in range(n)]
            Us = [_dot(Pm[c], Xs[c]) for c in range(n)]
            Xs = [Xs[c] - Us[c].astype(BF16) for c in range(n)]
        else:
            Ps = [_dot(_sibling_rows(Xs[c], b, 1 - dirs[c]), Abs[c]) for c in range(n)]
            Pm = [Ps[c].astype(BF16) * lmh_s[dirs[c], p] for c in range(n)]
            Us = [_dot(Pm[c], Xs[c]).astype(BF16) for c in range(n)]
            for c in range(n):
                d = dirs[c]
                sib = 1 - d
                pieces = []
                for m in range(SUPER // (2 * b)):
                    keep = Xs[c][(2 * m + 1 - sib) * b:(2 * m + 2 - sib) * b]
                    new = Xs[c][(2 * m + sib) * b:(2 * m + sib + 1) * b] - Us[c][m * b:(m + 1) * b]
                    pieces += [new, keep] if d else [keep, new]
                Xs[c] = jnp.concatenate(pieces, axis=0)
    return Xs


def _delta_body(alog_ref, dtb_ref, q_ref, k_ref, v_ref, z_ref, cwq_ref, cwk_ref, cwv_ref, ab_ref,
                nw_ref, *rest, T, latent, hp):
    if latent:
        s0_ref, o_ref, eye_s, lm_s, lmh_s, tri_s, gt_s, oacc_s = rest
    else:
        o_ref, st_ref, eye_s, lm_s, lmh_s, tri_s, gt_s, oacc_s = rest
    nsup = T // SUPER

    @pl.when((pl.program_id(0) == 0) & (pl.program_id(1) == 0))
    def _init_masks():
        ii = lax.broadcasted_iota(jnp.int32, (SUPER, SUPER), 0)
        jj = lax.broadcasted_iota(jnp.int32, (SUPER, SUPER), 1)
        eye_s[...] = jnp.where(ii == jj, 1.0, 0.0).astype(BF16)
        xor = ii ^ jj
        lvl = jnp.zeros((SUPER, SUPER), jnp.int32)
        for p in range(1, N_LEVELS):
            lvl = lvl + (xor >= (1 << p)).astype(jnp.int32)
        for d in range(2):
            strict = (ii < jj) if d else (ii > jj)
            for p in range(N_LEVELS):
                m = jnp.where(strict & (lvl == p), 1.0, 0.0).astype(BF16)
                lm_s[d, p] = m
                if (1 << p) >= BF16_ROWS:
                    lmh_s[d, p] = _sibling_rows(m, 1 << p, 1 - d)
            tri_s[d] = jnp.where((ii <= jj) if d else (ii >= jj), 1.0, 0.0).astype(BF16)

    trow = lax.broadcasted_iota(jnp.int32, (T, 1), 0)
    pos_c = trow & (SUPER - 1)
    lane = lax.broadcasted_iota(jnp.int32, (T, LANES), 1)

    ab = ab_ref[...]
    g_all = -jnp.exp(alog_ref[...]) * _softplus(ab + dtb_ref[...])
    beta_all = _sigmoid(ab)
    pre = _seg_cumsum(g_all, pos_c, 0, False)
    suf = _seg_cumsum(g_all, pos_c, 0, True)
    G_all = jnp.where(lane >= DEL_HEADS, suf, pre)
    tot_all = pre + suf - g_all
    gt_s[...] = G_all.T

    def column(x, col):
        return jnp.sum(jnp.where(lane == col, x, 0.0), axis=1, keepdims=True)

    def conv(x, w):
        xp = jnp.where(trow >= 1, pltpu.roll(x, 1, 0), 0.0)
        xn = jnp.where(trow <= T - 2, pltpu.roll(x, T - 1, 0), 0.0)
        return _silu(w[0:1, :] * xp + w[1:2, :] * x + w[2:3, :] * xn)

    probs = []
    for hh in range(hp):
        head = pl.program_id(1) * hp + hh
        cq = slice(hh * DEL_DK, (hh + 1) * DEL_DK)
        cv = slice(hh * DEL_DV, (hh + 1) * DEL_DV)
        q = conv(q_ref[:, cq], cwq_ref[:, cq])
        k = conv(k_ref[:, cq], cwk_ref[:, cq])
        v = conv(v_ref[:, cv], cwv_ref[:, cv])
        q = q * lax.rsqrt(jnp.sum(q * q, axis=-1, keepdims=True) + EPS) * (DEL_DK ** -0.5)
        k = k * lax.rsqrt(jnp.sum(k * k, axis=-1, keepdims=True) + EPS)
        qb = q.astype(BF16)
        kb = k.astype(BF16)
        sup = [slice(r * SUPER, (r + 1) * SUPER) for r in range(nsup)]
        kks = [_dot_nt(kb[rows], kb[rows]) for rows in sup]
        qks = [_dot_nt(qb[rows], kb[rows]) for rows in sup]
        for d in range(2):
            col = d * DEL_HEADS + head
            beta = column(beta_all, 2 * DEL_HEADS + col)
            G = column(G_all, col)
            tot = column(tot_all, col)
            G_row = gt_s[pl.ds(col, 1), :]
            for r, rows in enumerate(sup):
                bt = beta[rows]
                Gc = G[rows]
                E = jnp.exp(jnp.minimum(Gc - G_row[:, rows], 0.0))
                pr = dict(hh=hh, d=d, r=r, rows=rows, cv=cv,
                          Ab=(bt * kks[r] * E).astype(BF16),
                          qkl=(qks[r] * E).astype(BF16) * tri_s[d],
                          vb=(v[rows] * bt).astype(BF16),
                          ke=(k[rows] * jnp.exp(tot[rows] - Gc)).astype(BF16))
                if latent:
                    eG = jnp.exp(Gc)
                    pr.update(kw=(k[rows] * (bt * eG)).astype(BF16), qe=q[rows] * eG,
                              et=jnp.exp(tot[r * SUPER:r * SUPER + 1]))
                probs.append(pr)

    Xs = _tri_inverses([pr["Ab"] for pr in probs], [pr["d"] for pr in probs], eye_s, lm_s, lmh_s)
    ubs = [_dot(X, pr["vb"]).astype(BF16) for X, pr in zip(Xs, probs)]
    o0s = [_dot(pr["qkl"], ub) for pr, ub in zip(probs, ubs)]
    Rs = [_dot_tn(pr["ke"], ub) for pr, ub in zip(probs, ubs)]
    for pr, o0 in zip(probs, o0s):
        if pr["d"]:
            oacc_s[pr["rows"], pr["cv"]] = oacc_s[pr["rows"], pr["cv"]] + o0
        else:
            oacc_s[pr["rows"], pr["cv"]] = o0

    if latent:
        wbs = [_dot(X, pr["kw"]).astype(BF16) for X, pr in zip(Xs, probs)]
        qts = [(pr["qe"] - _dot(pr["qkl"], wb)).astype(BF16) for pr, wb in zip(probs, wbs)]
        kms = [_dot_tn(pr["ke"], wb).astype(BF16) for pr, wb in zip(probs, wbs)]
        by_key = {(pr["hh"], pr["d"], pr["r"]): c for c, pr in enumerate(probs)}
        states = {(hh, d): s0_ref[d, hh] for hh in range(hp) for d in range(2)}
        for step in range(nsup):
            for hh in range(hp):
                for d in range(2):
                    c = by_key[(hh, d, nsup - 1 - step if d else step)]
                    pr = probs[c]
                    S = states[(hh, d)]
                    Sb = S.astype(BF16)
                    oacc_s[pr["rows"], pr["cv"]] = oacc_s[pr["rows"], pr["cv"]] + _dot(qts[c], Sb)
                    states[(hh, d)] = S * pr["et"] + Rs[c] - _dot(kms[c], Sb)
    else:
        for pr, R in zip(probs, Rs):
            st_ref[pr["d"], pr["hh"]] = R

    for hh in range(hp):
        cv = slice(hh * DEL_DV, (hh + 1) * DEL_DV)
        o = oacc_s[:, cv]
        o = o * lax.rsqrt(jnp.mean(o * o, axis=-1, keepdims=True) + EPS)
        o_ref[:, cv] = (o * nw_ref[:, cv] * _silu(z_ref[:, cv])).astype(o_ref.dtype)


def _delta_mix(proj, ab, alog, dtb, convw, nw, T, *, j, hp, s0=None, state_in=None):
    n = proj.shape[0]
    nseq = n // T
    latent = s0 is not None
    H = DEL_HEADS
    ng = H // hp
    wk, wv = hp * DEL_DK, hp * DEL_DV
    alog = jnp.pad(alog.reshape(1, 2 * H), ((0, 0), (0, LANES - 2 * H)))
    dtb = jnp.pad(dtb.reshape(1, 2 * H), ((0, 0), (0, LANES - 2 * H)))
    row = pl.BlockSpec((1, LANES), lambda b, h: (0, 0))
    in_specs = [
        row, row,
        pl.BlockSpec((T, wk), lambda b, h: (b, h)),
        pl.BlockSpec((T, wk), lambda b, h: (b, ng + h)),
        pl.BlockSpec((T, wv), lambda b, h: (b, ng + h)),
        pl.BlockSpec((T, wv), lambda b, h: (b, 2 * ng + h)),
        pl.BlockSpec((3, wk), lambda b, h: (0, h)),
        pl.BlockSpec((3, wk), lambda b, h: (0, ng + h)),
        pl.BlockSpec((3, wv), lambda b, h: (0, ng + h)),
        pl.BlockSpec((T, LANES), lambda b, h: (b, 0)),
        pl.BlockSpec((1, wv), lambda b, h: (0, h)),
    ]
    args = [alog, dtb, proj, proj, proj, proj, convw, convw, convw, ab, nw]
    o_spec = pl.BlockSpec((T, wv), lambda b, h: (b, h))
    o_shape = jax.ShapeDtypeStruct((n, DEL_V), BF16)
    st_block = (None, None, 2, hp, DEL_DK, DEL_DV)
    aliases = {}
    scratch = [
        pltpu.VMEM((SUPER, SUPER), BF16),
        pltpu.VMEM((2, N_LEVELS, SUPER, SUPER), BF16),
        pltpu.VMEM((2, N_LEVELS, SUPER // 2, SUPER), BF16),
        pltpu.VMEM((2, SUPER, SUPER), BF16),
        pltpu.VMEM((LANES, T), F32),
        pltpu.VMEM((T, wv), F32),
    ]
    if latent:
        in_specs.append(pl.BlockSpec(st_block, lambda b, h: (b, j, 0, h, 0, 0)))
        args.append(s0)
        out_specs = o_spec
        out_shape = o_shape
    else:
        st_spec = pl.BlockSpec(st_block, lambda b, h: (b, j, 0, h, 0, 0))
        st_shape = jax.ShapeDtypeStruct((nseq, 2, 2, H, DEL_DK, DEL_DV), F32)
        out_specs = [o_spec, st_spec]
        out_shape = [o_shape, st_shape]
        if state_in is not None:
            in_specs.append(pl.BlockSpec(memory_space=pl.ANY))
            args.append(state_in)
            aliases = {len(args) - 1: 1}
    body = functools.partial(_delta_body, T=T, latent=latent, hp=hp)
    if state_in is not None:
        inner = body

        def body(*refs):
            n_in = len(args)
            return inner(*refs[:n_in - 1], *refs[n_in:])

    return pl.pallas_call(
        body,
        grid=(nseq, ng),
        in_specs=in_specs,
        out_specs=out_specs,
        out_shape=out_shape,
        scratch_shapes=scratch,
        input_output_aliases=aliases,
        compiler_params=pltpu.CompilerParams(vmem_limit_bytes=VMEM_LIMIT),
        name="delta_mix_lat" if latent else "delta_mix_ctx",
    )(*args)


def _rope_tables(T):
    rows = T // GRID_W
    r = jnp.broadcast_to(jnp.arange(rows)[:, None], (rows, GRID_W)).reshape(T).astype(F32)
    col = jnp.broadcast_to(jnp.arange(GRID_W)[None, :], (rows, GRID_W)).reshape(T).astype(F32)
    n_pairs = RET_DK // 4
    freqs = ROPE_BASE ** (-jnp.arange(n_pairs, dtype=F32) / n_pairs)
    ang = jnp.concatenate([r[:, None] * freqs, col[:, None] * freqs], -1)
    return jnp.cos(ang), jnp.sin(ang)


def kernel(x_prompt, x_sample, state_ret, state_delta, c, c_ctx, norm_w, mod_w, mod_b, ret_w_in, ret_decay,
           ret_gn_w, ret_w_out, del_w_in, del_conv_w, del_a_log, del_dt_bias, del_norm_w, del_w_out,
           final_norm_w):
    B, T_ctx, _ = x_prompt.shape
    Bd, T_lat, _ = x_sample.shape
    n_ctx = B * T_ctx
    n_lat = Bd * T_lat

    x_ctx = x_prompt.reshape(n_ctx, D_MODEL)
    x_lat = x_sample.reshape(n_lat, D_MODEL)

    cond8 = jnp.zeros((8, D_MODEL), F32).at[0].set(c_ctx).at[1:1 + Bd].set(c)
    mods = _modulation(cond8, mod_w, mod_b)
    cidx_ctx = jnp.zeros((n_ctx // ROW_BLOCK,), jnp.int32)
    cidx_lat = 1 + jnp.arange(n_lat // ROW_BLOCK, dtype=jnp.int32) // (T_lat // ROW_BLOCK)

    rope = _rope_tables(T_lat)
    st_ret = None
    st_del = None
    for i in range(DEPTH):
        j = i // 2
        mod = mods[i]
        nw = norm_w[i][None, :]
        last = i == DEPTH - 1
        fnw = final_norm_w[None, :] if last else None
        if i % 2 == 0:
            w_in = ret_w_in[j].astype(BF16)
            w_out = ret_w_out[j].astype(BF16)
            gnw = ret_gn_w[j][None, :]
            p_ctx = _inproj(x_ctx, cidx_ctx, mod, nw, w_in)
            p_lat = _inproj(x_lat, cidx_lat, mod, nw, w_in)
            o_ctx, st_ret = _ret_mix(p_ctx, ret_decay[j], gnw, T_ctx, j=j, state_in=st_ret)
            o_lat = _ret_mix(p_lat, ret_decay[j], gnw, T_lat, j=j, rope=rope, s0=state_ret)
        else:
            w_full = del_w_in[j]
            w_in = w_full[:, :DEL_MAIN].astype(BF16)
            w_ab = jnp.pad(w_full[:, DEL_MAIN:], ((0, 0), (0, LANES - DEL_AB))).astype(BF16)
            w_out = del_w_out[j].astype(BF16)
            dnw = del_norm_w[j][None, :]
            p_ctx, ab_ctx = _inproj(x_ctx, cidx_ctx, mod, nw, w_in, w_ab)
            p_lat, ab_lat = _inproj(x_lat, cidx_lat, mod, nw, w_in, w_ab)
            o_ctx, st_del = _delta_mix(p_ctx, ab_ctx, del_a_log[j], del_dt_bias[j], del_conv_w[j], dnw,
                                       T_ctx, j=j, hp=4, state_in=st_del)
            o_lat = _delta_mix(p_lat, ab_lat, del_a_log[j], del_dt_bias[j], del_conv_w[j], dnw,
                               T_lat, j=j, hp=1, s0=state_delta)
        x_ctx = _outproj(x_ctx, o_ctx, cidx_ctx, mod, w_out, fnw)
        x_lat = _outproj(x_lat, o_lat, cidx_lat, mod, w_out, fnw)

    y_prompt = x_ctx.reshape(B, T_ctx, D_MODEL)
    y_sample = x_lat.reshape(Bd, T_lat, D_MODEL)
    return (y_prompt, y_sample, st_ret, st_del)
```

```python
import functools

import jax
import jax.numpy as jnp
from jax import lax
from jax.experimental import pallas as pl
from jax.experimental.pallas import tpu as pltpu

F32 = jnp.float32
BF16 = jnp.bfloat16

D_MODEL = 1024
DEPTH = 4
GRID_W = 64
EPS = 1e-6
ROPE_BASE = 10000.0
CONV_W = 3

RET_HEADS = 4
RET_DK = 256
RET_DV = 512
RET_QK = RET_HEADS * RET_DK
RET_V = RET_HEADS * RET_DV
RET_IN = 2 * RET_QK + 2 * RET_V

DEL_HEADS = 8
DEL_DK = 128
DEL_DV = 256
DEL_QK = DEL_HEADS * DEL_DK
DEL_V = DEL_HEADS * DEL_DV
DEL_CONV = 2 * DEL_QK + DEL_V
DEL_MAIN = DEL_CONV + DEL_V
DEL_AB = 4 * DEL_HEADS
assert RET_IN == DEL_MAIN

ROW_BLOCK = 256
COL_GROUP = 1024
SUPER = 256
N_LEVELS = SUPER.bit_length() - 1
LANES = 128
BF16_ROWS = 16
VMEM_LIMIT = 56 * 1024 * 1024


def _sigmoid(x):
    return 1.0 / (1.0 + jnp.exp(-x))


def _silu(x):
    return x * _sigmoid(x)


def _softplus(x):
    return jnp.maximum(x, 0.0) + jnp.log1p(jnp.exp(-jnp.abs(x)))


def _log_sigmoid(x):
    return jnp.minimum(x, 0.0) - jnp.log1p(jnp.exp(-jnp.abs(x)))


def _dot(a, b):
    return jnp.dot(a, b, preferred_element_type=F32)


def _dot_nt(a, b):
    return lax.dot_general(a, b, (((1,), (1,)), ((), ())), preferred_element_type=F32)


def _dot_tn(a, b):
    return lax.dot_general(a, b, (((0,), (0,)), ((), ())), preferred_element_type=F32)


def _short_conv(x, w, trow, T):
    xp = jnp.where(trow >= 1, pltpu.roll(x, 1, 0), 0.0)
    xn = jnp.where(trow <= T - 2, pltpu.roll(x, T - 1, 0), 0.0)
    return w[0:1, :] * xp + w[1:2, :] * x + w[2:3, :] * xn


def _l2norm_heads(x, width, scale):
    out = []
    for s in range(0, x.shape[1], width):
        xs = x[:, s:s + width]
        out.append(xs * (lax.rsqrt(jnp.sum(xs * xs, axis=-1, keepdims=True) + EPS) * scale))
    return jnp.concatenate(out, axis=-1)


def _mod_body(cond_ref, w_ref, b_ref, o_ref):
    sc = _silu(cond_ref[...])
    o_ref[...] = jnp.dot(sc, w_ref[...], preferred_element_type=F32,
                         precision=lax.Precision.HIGHEST) + b_ref[...]


def _modulation(cond8, mod_w, mod_b):
    nb = 3
    return pl.pallas_call(
        _mod_body,
        grid=(DEPTH, nb),
        in_specs=[
            pl.BlockSpec((8, D_MODEL), lambda i, j: (0, 0)),
            pl.BlockSpec((None, D_MODEL, D_MODEL), lambda i, j: (i, 0, j)),
            pl.BlockSpec((None, 1, D_MODEL), lambda i, j: (i, 0, j)),
        ],
        out_specs=pl.BlockSpec((None, 8, D_MODEL), lambda i, j: (i, 0, j)),
        out_shape=jax.ShapeDtypeStruct((DEPTH, 8, 3 * D_MODEL), F32),
        compiler_params=pltpu.CompilerParams(vmem_limit_bytes=VMEM_LIMIT),
        name="modulation",
    )(cond8, mod_w, mod_b.reshape(DEPTH, 1, 3 * D_MODEL))


def _inproj_body(cidx_ref, x_ref, mod_ref, nw_ref, w_ref, *rest, mode):
    if mode == "ret":
        (o_ref,) = rest
    elif mode == "ret_rope":
        cos_ref, sin_ref, o_ref = rest
    elif mode == "del_conv":
        wab_ref, cw_ref, o_ref, ab_ref = rest
    else:
        wab_ref, o_ref, ab_ref = rest
    ci = cidx_ref[pl.program_id(0)]
    x = x_ref[...]
    ms = jnp.mean(x * x, axis=-1, keepdims=True)
    xn = x * lax.rsqrt(ms + EPS) * nw_ref[...]
    m = mod_ref[pl.ds(ci, 1), :]
    shift = m[:, :D_MODEL]
    scale = m[:, D_MODEL:2 * D_MODEL]
    h = (xn * (1.0 + scale) + shift).astype(BF16)
    if mode.startswith("del"):
        ab_ref[...] = _dot(h, wab_ref[...])
    trow = lax.broadcasted_iota(jnp.int32, (ROW_BLOCK, 1), 0)
    gate_start = 2 * RET_QK + RET_V
    for c0 in range(0, RET_IN, COL_GROUP):
        cols = slice(c0, c0 + COL_GROUP)
        p = _dot(h, w_ref[:, cols])
        if c0 >= gate_start:
            p = _silu(p)
        elif mode == "ret_rope" and c0 < 2 * RET_QK:
            cos = cos_ref[...]
            sin = sin_ref[...]
            hk = RET_DK // 2
            out = []
            for s in range(0, COL_GROUP, RET_DK):
                x1, x2 = p[:, s:s + hk], p[:, s + hk:s + RET_DK]
                out += [x1 * cos - x2 * sin, x1 * sin + x2 * cos]
            p = jnp.concatenate(out, axis=-1)
        elif mode == "del_conv":
            p = _silu(_short_conv(p, cw_ref[:, cols], trow, ROW_BLOCK))
            if c0 < DEL_QK:
                p = _l2norm_heads(p, DEL_DK, DEL_DK ** -0.5)
            elif c0 < 2 * DEL_QK:
                p = _l2norm_heads(p, DEL_DK, 1.0)
        o_ref[:, cols] = p.astype(BF16)


def _inproj(x, cidx, mods, norm_w3, w, layer, j, mode, *, wab=None, convw=None, rope=None, blocks_per_seq=1):
    n = x.shape[0]
    ncols = w.shape[2]
    in_specs = [
        pl.BlockSpec((ROW_BLOCK, D_MODEL), lambda i, c: (i, 0)),
        pl.BlockSpec((None, 8, 3 * D_MODEL), lambda i, c: (layer, 0, 0)),
        pl.BlockSpec((None, 1, D_MODEL), lambda i, c: (layer, 0, 0)),
        pl.BlockSpec((None, D_MODEL, ncols), lambda i, c: (j, 0, 0)),
    ]
    out_specs = [pl.BlockSpec((ROW_BLOCK, RET_IN), lambda i, c: (i, 0))]
    out_shape = [jax.ShapeDtypeStruct((n, RET_IN), BF16)]
    args = [x, mods, norm_w3, w]
    if mode == "ret_rope":
        cos, sin = rope
        spec = pl.BlockSpec((ROW_BLOCK, LANES), lambda i, c: (i % blocks_per_seq, 0))
        in_specs += [spec, spec]
        args += [cos, sin]
    if mode.startswith("del"):
        in_specs.append(pl.BlockSpec((None, D_MODEL, LANES), lambda i, c: (j, 0, 0)))
        args.append(wab)
        if mode == "del_conv":
            in_specs.append(pl.BlockSpec((None, CONV_W, DEL_CONV), lambda i, c: (j, 0, 0)))
            args.append(convw)
        out_specs.append(pl.BlockSpec((ROW_BLOCK, LANES), lambda i, c: (i, 0)))
        out_shape.append(jax.ShapeDtypeStruct((n, LANES), F32))
    res = pl.pallas_call(
        functools.partial(_inproj_body, mode=mode),
        grid_spec=pltpu.PrefetchScalarGridSpec(
            num_scalar_prefetch=1, grid=(n // ROW_BLOCK,), in_specs=in_specs, out_specs=out_specs),
        out_shape=out_shape,
        compiler_params=pltpu.CompilerParams(vmem_limit_bytes=VMEM_LIMIT),
        name="inproj_" + mode,
    )(cidx, *args)
    return res if mode.startswith("del") else res[0]


def _outproj_body(cidx_ref, x_ref, o_ref, mod_ref, w_ref, *rest, final):
    if final:
        fnw_ref, out_ref = rest
    else:
        (out_ref,) = rest
    ci = cidx_ref[pl.program_id(0)]
    gate = mod_ref[pl.ds(ci, 1), :][:, 2 * D_MODEL:]
    xn = x_ref[...] + gate * _dot(o_ref[...], w_ref[...])
    if final:
        ms = jnp.mean(xn * xn, axis=-1, keepdims=True)
        xn = xn * lax.rsqrt(ms + EPS) * fnw_ref[...]
    out_ref[...] = xn


def _outproj(x, o, cidx, mods, w, layer, j, fnw=None):
    n = x.shape[0]
    final = fnw is not None
    vdim = o.shape[1]
    in_specs = [
        pl.BlockSpec((ROW_BLOCK, D_MODEL), lambda i, c: (i, 0)),
        pl.BlockSpec((ROW_BLOCK, vdim), lambda i, c: (i, 0)),
        pl.BlockSpec((None, 8, 3 * D_MODEL), lambda i, c: (layer, 0, 0)),
        pl.BlockSpec((None, vdim, D_MODEL), lambda i, c: (j, 0, 0)),
    ]
    args = [x, o, mods, w]
    if final:
        in_specs.append(pl.BlockSpec((1, D_MODEL), lambda i, c: (0, 0)))
        args.append(fnw)
    return pl.pallas_call(
        functools.partial(_outproj_body, final=final),
        grid_spec=pltpu.PrefetchScalarGridSpec(
            num_scalar_prefetch=1, grid=(n // ROW_BLOCK,), in_specs=in_specs,
            out_specs=pl.BlockSpec((ROW_BLOCK, D_MODEL), lambda i, c: (i, 0))),
        out_shape=jax.ShapeDtypeStruct((n, D_MODEL), F32),
        compiler_params=pltpu.CompilerParams(vmem_limit_bytes=VMEM_LIMIT),
        name="outproj",
    )(cidx, *args)


def _ret_body(decay_ref, q_ref, k_ref, v_ref, z_ref, gnw_ref, *rest, T, latent):
    if latent:
        s0_ref, o_ref, m_s = rest
    else:
        o_ref, st_ref, m_s = rest
    hd = pl.program_id(0)
    lgf = _log_sigmoid(jnp.full((1, 1), decay_ref[0, hd], F32))
    lgb = _log_sigmoid(jnp.full((1, 1), decay_ref[1, hd], F32))
    scale = RET_DK ** -0.5
    QT = 256

    @pl.when(pl.program_id(1) == 0)
    def _build_mask():
        for qi in range(T // QT):
            ti = lax.broadcasted_iota(jnp.int32, (QT, T), 0) + qi * QT
            tj = lax.broadcasted_iota(jnp.int32, (QT, T), 1)
            dd = (ti - tj).astype(F32)
            lg = jnp.where(dd > 0, lgf, lgb)
            m_s[qi * QT:(qi + 1) * QT, :] = jnp.where(dd == 0, 2.0 * scale, jnp.exp(lg * jnp.abs(dd)) * scale)

    q = q_ref[...]
    k = k_ref[...]
    v = v_ref[...]
    tcol = lax.broadcasted_iota(jnp.int32, (T, 1), 0).astype(F32)
    if latent:
        s0f = s0_ref[0].astype(BF16)
        s0b = s0_ref[1].astype(BF16)

    gnw = gnw_ref[...]
    for qi in range(T // QT):
        rows = slice(qi * QT, (qi + 1) * QT)
        qt = q[rows]
        s = _dot_nt(qt, k) * m_s[rows, :]
        o = _dot(s.astype(BF16), v)
        if latent:
            tq = tcol[rows]
            qf = qt.astype(F32)
            xif = jnp.exp(lgf * (tq + 1.0))
            xib = jnp.exp(lgb * (float(T) - tq))
            o = o + _dot((qf * xif).astype(BF16), s0f) + _dot((qf * xib).astype(BF16), s0b)
        mu = jnp.mean(o, axis=-1, keepdims=True)
        oc = o - mu
        var = jnp.mean(oc * oc, axis=-1, keepdims=True)
        on = oc * lax.rsqrt(var + EPS) * gnw
        o_ref[rows, :] = (on * z_ref[rows, :].astype(F32)).astype(o_ref.dtype)

    if not latent:
        kf = k.astype(F32)
        zf = jnp.exp(lgf * (float(T - 1) - tcol)) * scale
        zb = jnp.exp(lgb * tcol) * scale
        st_ref[0] = _dot_tn((kf * zf).astype(BF16), v)
        st_ref[1] = _dot_tn((kf * zb).astype(BF16), v)


def _ret_mix(proj, decay, gnw, T, *, j, latent, s0=None, state_in=None):
    n = proj.shape[0]
    nseq = n // T
    smem = pl.BlockSpec(memory_space=pltpu.SMEM)
    in_specs = [
        smem,
        pl.BlockSpec((T, RET_DK), lambda h, b: (b, h)),
        pl.BlockSpec((T, RET_DK), lambda h, b: (b, RET_HEADS + h)),
        pl.BlockSpec((T, RET_DV), lambda h, b: (b, RET_HEADS + h)),
        pl.BlockSpec((T, RET_DV), lambda h, b: (b, 2 * RET_HEADS + h)),
        pl.BlockSpec((1, RET_DV), lambda h, b: (0, h)),
    ]
    args = [decay, proj, proj, proj, proj, gnw]
    o_spec = pl.BlockSpec((T, RET_DV), lambda h, b: (b, h))
    o_shape = jax.ShapeDtypeStruct((n, RET_V), BF16)
    aliases = {}
    if latent:
        in_specs.append(pl.BlockSpec((None, None, 2, None, RET_DK, RET_DV), lambda h, b: (b, j, 0, h, 0, 0)))
        args.append(s0)
        out_specs = o_spec
        out_shape = o_shape
    else:
        st_spec = pl.BlockSpec((None, None, 2, None, RET_DK, RET_DV), lambda h, b: (b, j, 0, h, 0, 0))
        st_shape = jax.ShapeDtypeStruct((nseq, 2, 2, RET_HEADS, RET_DK, RET_DV), F32)
        out_specs = [o_spec, st_spec]
        out_shape = [o_shape, st_shape]
        if state_in is not None:
            in_specs.append(pl.BlockSpec(memory_space=pl.ANY))
            args.append(state_in)
            aliases = {len(args) - 1: 1}
    body = functools.partial(_ret_body, T=T, latent=latent)
    if state_in is not None:
        inner = body

        def body(*refs):
            n_in = len(args)
            return inner(*refs[:n_in - 1], *refs[n_in:])

    return pl.pallas_call(
        body,
        grid=(RET_HEADS, nseq),
        in_specs=in_specs,
        out_specs=out_specs,
        out_shape=out_shape,
        scratch_shapes=[pltpu.VMEM((T, T), F32)],
        input_output_aliases=aliases,
        compiler_params=pltpu.CompilerParams(vmem_limit_bytes=VMEM_LIMIT),
        name="ret_mix_lat" if latent else "ret_mix_ctx",
    )(*args)


def _seg_cumsum(x, pos, axis, reverse):
    n = x.shape[axis]
    s = 1
    while s < SUPER:
        if reverse:
            x = x + jnp.where(pos < SUPER - s, pltpu.roll(x, n - s, axis), 0.0)
        else:
            x = x + jnp.where(pos >= s, pltpu.roll(x, s, axis), 0.0)
        s *= 2
    return x


def _sibling_rows(x, b, sib):
    return jnp.concatenate([x[(2 * m + sib) * b:(2 * m + sib + 1) * b] for m in range(x.shape[0] // (2 * b))], axis=0)


def _tri_inverses(Abs, dirs, eye_s, lm_s, lmh_s):
    n = len(Abs)
    Xs = [eye_s[...] - Abs[c] * lm_s[dirs[c], 0] for c in range(n)]
    for p in range(1, N_LEVELS):
        b = 1 << p
        if b < BF16_ROWS:
            Ps = [_dot(Xs[c], Abs[c]) for c in range(n)]
            Pm = [Ps[c].astype(BF16) * lm_s[dirs[c], p] for c in range(n)]
            Us = [_dot(Pm[c], Xs[c]) for c in range(n)]
            Xs = [Xs[c] - Us[c].astype(BF16) for c in range(n)]
        else:
            Ps = [_dot(_sibling_rows(Xs[c], b, 1 - dirs[c]), Abs[c]) for c in range(n)]
            Pm = [Ps[c].astype(BF16) * lmh_s[dirs[c], p] for c in range(n)]
            Us = [_dot(Pm[c], Xs[c]).astype(BF16) for c in range(n)]
            for c in range(n):
                d = dirs[c]
                sib = 1 - d
                pieces = []
                for m in range(SUPER // (2 * b)):
                    keep = Xs[c][(2 * m + 1 - sib) * b:(2 * m + 2 - sib) * b]
                    new = Xs[c][(2 * m + sib) * b:(2 * m + sib + 1) * b] - Us[c][m * b:(m + 1) * b]
                    pieces += [new, keep] if d else [keep, new]
                Xs[c] = jnp.concatenate(pieces, axis=0)
    return Xs


def _delta_body(alog_ref, dtb_ref, q_ref, k_ref, v_ref, z_ref, ab_ref, nw_ref, *rest, T, latent, preconv, hp):
    if not preconv:
        cwq_ref, cwk_ref, cwv_ref = rest[:3]
        rest = rest[3:]
    if latent:
        s0_ref, o_ref, eye_s, lm_s, lmh_s, tri_s, gt_s, oacc_s = rest
    else:
        o_ref, st_ref, eye_s, lm_s, lmh_s, tri_s, gt_s, oacc_s = rest
    nsup = T // SUPER

    @pl.when((pl.program_id(0) == 0) & (pl.program_id(1) == 0))
    def _init_masks():
        ii = lax.broadcasted_iota(jnp.int32, (SUPER, SUPER), 0)
        jj = lax.broadcasted_iota(jnp.int32, (SUPER, SUPER), 1)
        eye_s[...] = jnp.where(ii == jj, 1.0, 0.0).astype(BF16)
        xor = ii ^ jj
        lvl = jnp.zeros((SUPER, SUPER), jnp.int32)
        for p in range(1, N_LEVELS):
            lvl = lvl + (xor >= (1 << p)).astype(jnp.int32)
        for d in range(2):
            strict = (ii < jj) if d else (ii > jj)
            for p in range(N_LEVELS):
                m = jnp.where(strict & (lvl == p), 1.0, 0.0).astype(BF16)
                lm_s[d, p] = m
                if (1 << p) >= BF16_ROWS:
                    lmh_s[d, p] = _sibling_rows(m, 1 << p, 1 - d)
            tri_s[d] = jnp.where((ii <= jj) if d else (ii >= jj), 1.0, 0.0).astype(BF16)

    trow = lax.broadcasted_iota(jnp.int32, (T, 1), 0)
    pos_c = trow & (SUPER - 1)
    lane = lax.broadcasted_iota(jnp.int32, (T, LANES), 1)

    ab = ab_ref[...]
    g_all = -jnp.exp(alog_ref[...]) * _softplus(ab + dtb_ref[...])
    beta_all = _sigmoid(ab)
    pre = _seg_cumsum(g_all, pos_c, 0, False)
    suf = _seg_cumsum(g_all, pos_c, 0, True)
    G_all = jnp.where(lane >= DEL_HEADS, suf, pre)
    tot_all = pre + suf - g_all
    gt_s[...] = G_all.T

    def column(x, col):
        return jnp.sum(jnp.where(lane == col, x, 0.0), axis=1, keepdims=True)

    probs = []
    for hh in range(hp):
        head = pl.program_id(1) * hp + hh
        cq = slice(hh * DEL_DK, (hh + 1) * DEL_DK)
        cv = slice(hh * DEL_DV, (hh + 1) * DEL_DV)
        if preconv:
            qb = q_ref[:, cq]
            kb = k_ref[:, cq]
            q = qb.astype(F32)
            k = kb.astype(F32)
            v = v_ref[:, cv].astype(F32)
        else:
            q = _silu(_short_conv(q_ref[:, cq].astype(F32), cwq_ref[:, cq], trow, T))
            k = _silu(_short_conv(k_ref[:, cq].astype(F32), cwk_ref[:, cq], trow, T))
            v = _silu(_short_conv(v_ref[:, cv].astype(F32), cwv_ref[:, cv], trow, T))
            q = _l2norm_heads(q, DEL_DK, DEL_DK ** -0.5)
            k = _l2norm_heads(k, DEL_DK, 1.0)
            qb = q.astype(BF16)
            kb = k.astype(BF16)
        sup = [slice(r * SUPER, (r + 1) * SUPER) for r in range(nsup)]
        kks = [_dot_nt(kb[rows], kb[rows]) for rows in sup]
        qks = [_dot_nt(qb[rows], kb[rows]) for rows in sup]
        for d in range(2):
            col = d * DEL_HEADS + head
            beta = column(beta_all, 2 * DEL_HEADS + col)
            G = column(G_all, col)
            tot = column(tot_all, col)
            G_row = gt_s[pl.ds(col, 1), :]
            for r, rows in enumerate(sup):
                bt = beta[rows]
                Gc = G[rows]
                E = jnp.exp(jnp.minimum(Gc - G_row[:, rows], 0.0))
                pr = dict(hh=hh, d=d, r=r, rows=rows, cv=cv,
                          Ab=(bt * kks[r] * E).astype(BF16),
                          qkl=(qks[r] * E).astype(BF16) * tri_s[d],
                          vb=(v[rows] * bt).astype(BF16),
                          ke=(k[rows] * jnp.exp(tot[rows] - Gc)).astype(BF16))
                if latent:
                    eG = jnp.exp(Gc)
                    pr.update(kw=(k[rows] * (bt * eG)).astype(BF16), qe=q[rows] * eG,
                              et=jnp.exp(tot[r * SUPER:r * SUPER + 1]))
                probs.append(pr)

    Xs = _tri_inverses([pr["Ab"] for pr in probs], [pr["d"] for pr in probs], eye_s, lm_s, lmh_s)
    ubs = [_dot(X, pr["vb"]).astype(BF16) for X, pr in zip(Xs, probs)]
    o0s = [_dot(pr["qkl"], ub) for pr, ub in zip(probs, ubs)]
    Rs = [_dot_tn(pr["ke"], ub) for pr, ub in zip(probs, ubs)]
    for pr, o0 in zip(probs, o0s):
        if pr["d"]:
            oacc_s[pr["rows"], pr["cv"]] = oacc_s[pr["rows"], pr["cv"]] + o0
        else:
            oacc_s[pr["rows"], pr["cv"]] = o0

    if latent:
        wbs = [_dot(X, pr["kw"]).astype(BF16) for X, pr in zip(Xs, probs)]
        qts = [(pr["qe"] - _dot(pr["qkl"], wb)).astype(BF16) for pr, wb in zip(probs, wbs)]
        kms = [_dot_tn(pr["ke"], wb).astype(BF16) for pr, wb in zip(probs, wbs)]
        by_key = {(pr["hh"], pr["d"], pr["r"]): c for c, pr in enumerate(probs)}
        states = {(hh, d): s0_ref[d, hh] for hh in range(hp) for d in range(2)}
        for step in range(nsup):
            for hh in range(hp):
                for d in range(2):
                    c = by_key[(hh, d, nsup - 1 - step if d else step)]
                    pr = probs[c]
                    S = states[(hh, d)]
                    Sb = S.astype(BF16)
                    oacc_s[pr["rows"], pr["cv"]] = oacc_s[pr["rows"], pr["cv"]] + _dot(qts[c], Sb)
                    states[(hh, d)] = S * pr["et"] + Rs[c] - _dot(kms[c], Sb)
    else:
        for pr, R in zip(probs, Rs):
            st_ref[pr["d"], pr["hh"]] = R

    for hh in range(hp):
        cv = slice(hh * DEL_DV, (hh + 1) * DEL_DV)
        o = oacc_s[:, cv]
        o = o * lax.rsqrt(jnp.mean(o * o, axis=-1, keepdims=True) + EPS)
        o_ref[:, cv] = (o * nw_ref[:, cv] * z_ref[:, cv].astype(F32)).astype(o_ref.dtype)


def _delta_mix(proj, ab, alog, dtb, convw, nw, T, *, j, hp, latent, preconv, s0=None, state_in=None):
    n = proj.shape[0]
    nseq = n // T
    H = DEL_HEADS
    ng = H // hp
    wk, wv = hp * DEL_DK, hp * DEL_DV
    alog = jnp.pad(alog.reshape(1, 2 * H), ((0, 0), (0, LANES - 2 * H)))
    dtb = jnp.pad(dtb.reshape(1, 2 * H), ((0, 0), (0, LANES - 2 * H)))
    row = pl.BlockSpec((1, LANES), lambda b, h: (0, 0))
    in_specs = [
        row, row,
        pl.BlockSpec((T, wk), lambda b, h: (b, h)),
        pl.BlockSpec((T, wk), lambda b, h: (b, ng + h)),
        pl.BlockSpec((T, wv), lambda b, h: (b, ng + h)),
        pl.BlockSpec((T, wv), lambda b, h: (b, 2 * ng + h)),
        pl.BlockSpec((T, LANES), lambda b, h: (b, 0)),
        pl.BlockSpec((1, wv), lambda b, h: (0, h)),
    ]
    args = [alog, dtb, proj, proj, proj, proj, ab, nw]
    if not preconv:
        in_specs += [
            pl.BlockSpec((None, CONV_W, wk), lambda b, h: (j, 0, h)),
            pl.BlockSpec((None, CONV_W, wk), lambda b, h: (j, 0, ng + h)),
            pl.BlockSpec((None, CONV_W, wv), lambda b, h: (j, 0, ng + h)),
        ]
        args += [convw, convw, convw]
    o_spec = pl.BlockSpec((T, wv), lambda b, h: (b, h))
    o_shape = jax.ShapeDtypeStruct((n, DEL_V), BF16)
    st_block = (None, None, 2, hp, DEL_DK, DEL_DV)
    aliases = {}
    scratch = [
        pltpu.VMEM((SUPER, SUPER), BF16),
        pltpu.VMEM((2, N_LEVELS, SUPER, SUPER), BF16),
        pltpu.VMEM((2, N_LEVELS, SUPER // 2, SUPER), BF16),
        pltpu.VMEM((2, SUPER, SUPER), BF16),
        pltpu.VMEM((LANES, T), F32),
        pltpu.VMEM((T, wv), F32),
    ]
    if latent:
        in_specs.append(pl.BlockSpec(st_block, lambda b, h: (b, j, 0, h, 0, 0)))
        args.append(s0)
        out_specs = o_spec
        out_shape = o_shape
    else:
        st_spec = pl.BlockSpec(st_block, lambda b, h: (b, j, 0, h, 0, 0))
        st_shape = jax.ShapeDtypeStruct((nseq, 2, 2, H, DEL_DK, DEL_DV), F32)
        out_specs = [o_spec, st_spec]
        out_shape = [o_shape, st_shape]
        if state_in is not None:
            in_specs.append(pl.BlockSpec(memory_space=pl.ANY))
            args.append(state_in)
            aliases = {len(args) - 1: 1}
    body = functools.partial(_delta_body, T=T, latent=latent, preconv=preconv, hp=hp)
    if state_in is not None:
        inner = body

        def body(*refs):
            n_in = len(args)
            return inner(*refs[:n_in - 1], *refs[n_in:])

    return pl.pallas_call(
        body,
        grid=(nseq, ng),
        in_specs=in_specs,
        out_specs=out_specs,
        out_shape=out_shape,
        scratch_shapes=scratch,
        input_output_aliases=aliases,
        compiler_params=pltpu.CompilerParams(vmem_limit_bytes=VMEM_LIMIT),
        name="delta_mix_lat" if latent else "delta_mix_ctx",
    )(*args)


def _rope_tables(T):
    rows = T // GRID_W
    r = jnp.broadcast_to(jnp.arange(rows)[:, None], (rows, GRID_W)).reshape(T).astype(F32)
    col = jnp.broadcast_to(jnp.arange(GRID_W)[None, :], (rows, GRID_W)).reshape(T).astype(F32)
    n_pairs = RET_DK // 4
    freqs = ROPE_BASE ** (-jnp.arange(n_pairs, dtype=F32) / n_pairs)
    ang = jnp.concatenate([r[:, None] * freqs, col[:, None] * freqs], -1)
    return jnp.cos(ang), jnp.sin(ang)


def _prepare_params(norm_w, ret_w_in, ret_w_out, del_w_in, del_w_out, final_norm_w):
    return dict(
        norm_w3=norm_w[:, None, :],
        ret_w_in=ret_w_in.astype(BF16),
        ret_w_out=ret_w_out.astype(BF16),
        del_w_in=del_w_in.astype(BF16),
        del_w_ab=jnp.pad(del_w_in[:, :, DEL_MAIN:], ((0, 0), (0, 0), (0, LANES - DEL_AB))).astype(BF16),
        del_w_out=del_w_out.astype(BF16),
        fnw=final_norm_w[None, :],
    )


def _group_trunk(x, cidx, T, latent, mods, prm, ret_decay, ret_gn_w, del_conv_w, del_a_log, del_dt_bias, del_norm_w,
                 state_ret=None, state_delta=None, depth=DEPTH):
    rope = _rope_tables(T) if latent else None
    preconv = (not latent) and T == ROW_BLOCK
    st_ret = None
    st_del = None
    for i in range(depth):
        j = i // 2
        fnw = prm["fnw"] if i == depth - 1 else None
        if i % 2 == 0:
            proj = _inproj(x, cidx, mods, prm["norm_w3"], prm["ret_w_in"], i, j, "ret_rope" if latent else "ret",
                           rope=rope, blocks_per_seq=T // ROW_BLOCK)
            gnw = ret_gn_w[j][None, :]
            if latent:
                o = _ret_mix(proj, ret_decay[j], gnw, T, j=j, latent=True, s0=state_ret)
            else:
                o, st_ret = _ret_mix(proj, ret_decay[j], gnw, T, j=j, latent=False, state_in=st_ret)
            w_out = prm["ret_w_out"]
        else:
            proj, ab = _inproj(x, cidx, mods, prm["norm_w3"], prm["del_w_in"], i, j,
                               "del_conv" if preconv else "del_raw", wab=prm["del_w_ab"], convw=del_conv_w)
            dnw = del_norm_w[j][None, :]
            if latent:
                o = _delta_mix(proj, ab, del_a_log[j], del_dt_bias[j], del_conv_w, dnw, T, j=j, hp=1,
                               latent=True, preconv=preconv, s0=state_delta)
            else:
                o, st_del = _delta_mix(proj, ab, del_a_log[j], del_dt_bias[j], del_conv_w, dnw, T, j=j, hp=4,
                                       latent=False, preconv=preconv, state_in=st_del)
            w_out = prm["del_w_out"]
        x = _outproj(x, o, cidx, mods, w_out, i, j, fnw)
    return x, st_ret, st_del


def kernel(x_prompt, x_sample, state_ret, state_delta, c, c_ctx, norm_w, mod_w, mod_b, ret_w_in, ret_decay,
           ret_gn_w, ret_w_out, del_w_in, del_conv_w, del_a_log, del_dt_bias, del_norm_w, del_w_out,
           final_norm_w):
    B, T_ctx, _ = x_prompt.shape
    Bd, T_lat, _ = x_sample.shape
    n_ctx = B * T_ctx
    n_lat = Bd * T_lat

    cond8 = jnp.zeros((8, D_MODEL), F32).at[0].set(c_ctx).at[1:1 + Bd].set(c)
    mods = _modulation(cond8, mod_w, mod_b)
    cidx_ctx = jnp.zeros((n_ctx // ROW_BLOCK,), jnp.int32)
    cidx_lat = 1 + jnp.arange(n_lat // ROW_BLOCK, dtype=jnp.int32) // (T_lat // ROW_BLOCK)
    prm = _prepare_params(norm_w, ret_w_in, ret_w_out, del_w_in, del_w_out, final_norm_w)
    shared = (mods, prm, ret_decay, ret_gn_w, del_conv_w, del_a_log, del_dt_bias, del_norm_w)

    y_ctx, st_ret, st_del = _group_trunk(x_prompt.reshape(n_ctx, D_MODEL), cidx_ctx, T_ctx, False, *shared)
    y_lat, _, _ = _group_trunk(x_sample.reshape(n_lat, D_MODEL), cidx_lat, T_lat, True, *shared,
                               state_ret=state_ret, state_delta=state_delta)
    return (y_ctx.reshape(B, T_ctx, D_MODEL), y_lat.reshape(Bd, T_lat, D_MODEL), st_ret, st_del)
```

```python
import functools

import jax
import jax.numpy as jnp
from jax import lax
from jax.experimental import pallas as pl
from jax.experimental.pallas import tpu as pltpu

F32 = jnp.float32
BF16 = jnp.bfloat16

D_MODEL = 1024
DEPTH = 4
GRID_W = 64
EPS = 1e-6
ROPE_BASE = 10000.0
CONV_W = 3

RET_HEADS = 4
RET_DK = 256
RET_DV = 512
RET_QK = RET_HEADS * RET_DK
RET_V = RET_HEADS * RET_DV
RET_IN = 2 * RET_QK + 2 * RET_V

DEL_HEADS = 8
DEL_DK = 128
DEL_DV = 256
DEL_QK = DEL_HEADS * DEL_DK
DEL_V = DEL_HEADS * DEL_DV
DEL_CONV = 2 * DEL_QK + DEL_V
DEL_MAIN = DEL_CONV + DEL_V
DEL_AB = 4 * DEL_HEADS
assert RET_IN == DEL_MAIN

ROW_BLOCK = 256
COL_GROUP = 1024
SUPER = 256
N_LEVELS = SUPER.bit_length() - 1
LANES = 128
BF16_ROWS = 16
VMEM_LIMIT = 56 * 1024 * 1024


def _sigmoid(x):
    return 1.0 / (1.0 + jnp.exp(-x))


def _silu(x):
    return x * _sigmoid(x)


def _softplus(x):
    return jnp.maximum(x, 0.0) + jnp.log1p(jnp.exp(-jnp.abs(x)))


def _log_sigmoid(x):
    return jnp.minimum(x, 0.0) - jnp.log1p(jnp.exp(-jnp.abs(x)))


def _dot(a, b):
    return jnp.dot(a, b, preferred_element_type=F32)


def _dot_nt(a, b):
    return lax.dot_general(a, b, (((1,), (1,)), ((), ())), preferred_element_type=F32)


def _dot_tn(a, b):
    return lax.dot_general(a, b, (((0,), (0,)), ((), ())), preferred_element_type=F32)


def _short_conv(x, w, trow, T):
    xp = jnp.where(trow >= 1, pltpu.roll(x, 1, 0), 0.0)
    xn = jnp.where(trow <= T - 2, pltpu.roll(x, T - 1, 0), 0.0)
    return w[0:1, :] * xp + w[1:2, :] * x + w[2:3, :] * xn


def _l2norm_heads(x, width, scale):
    out = []
    for s in range(0, x.shape[1], width):
        xs = x[:, s:s + width]
        out.append(xs * (lax.rsqrt(jnp.sum(xs * xs, axis=-1, keepdims=True) + EPS) * scale))
    return jnp.concatenate(out, axis=-1)


def _mod_body(cond_ref, w_ref, b_ref, o_ref):
    sc = _silu(cond_ref[...])
    o_ref[...] = jnp.dot(sc, w_ref[...], preferred_element_type=F32,
                         precision=lax.Precision.HIGHEST) + b_ref[...]


def _modulation(cond8, mod_w, mod_b):
    nb = 3
    return pl.pallas_call(
        _mod_body,
        grid=(DEPTH, nb),
        in_specs=[
            pl.BlockSpec((8, D_MODEL), lambda i, j: (0, 0)),
            pl.BlockSpec((None, D_MODEL, D_MODEL), lambda i, j: (i, 0, j)),
            pl.BlockSpec((None, 1, D_MODEL), lambda i, j: (i, 0, j)),
        ],
        out_specs=pl.BlockSpec((None, 8, D_MODEL), lambda i, j: (i, 0, j)),
        out_shape=jax.ShapeDtypeStruct((DEPTH, 8, 3 * D_MODEL), F32),
        compiler_params=pltpu.CompilerParams(vmem_limit_bytes=VMEM_LIMIT),
        name="modulation",
    )(cond8, mod_w, mod_b.reshape(DEPTH, 1, 3 * D_MODEL))


def _inproj_body(cidx_ref, x_ref, mod_ref, nw_ref, w_ref, *rest, mode):
    if mode == "ret":
        (o_ref,) = rest
    elif mode == "ret_rope":
        cos_ref, sin_ref, o_ref = rest
    elif mode == "del_conv":
        wab_ref, cw_ref, o_ref, ab_ref = rest
    else:
        wab_ref, o_ref, ab_ref = rest
    ci = cidx_ref[pl.program_id(0)]
    x = x_ref[...]
    ms = jnp.mean(x * x, axis=-1, keepdims=True)
    xn = x * lax.rsqrt(ms + EPS) * nw_ref[...]
    m = mod_ref[pl.ds(ci, 1), :]
    shift = m[:, :D_MODEL]
    scale = m[:, D_MODEL:2 * D_MODEL]
    h = (xn * (1.0 + scale) + shift).astype(BF16)
    if mode.startswith("del"):
        ab_ref[...] = _dot(h, wab_ref[...])
    trow = lax.broadcasted_iota(jnp.int32, (ROW_BLOCK, 1), 0)
    gate_start = 2 * RET_QK + RET_V
    for c0 in range(0, RET_IN, COL_GROUP):
        cols = slice(c0, c0 + COL_GROUP)
        p = _dot(h, w_ref[:, cols])
        if c0 >= gate_start:
            p = _silu(p)
        elif mode == "ret_rope" and c0 < 2 * RET_QK:
            cos = cos_ref[...]
            sin = sin_ref[...]
            hk = RET_DK // 2
            out = []
            for s in range(0, COL_GROUP, RET_DK):
                x1, x2 = p[:, s:s + hk], p[:, s + hk:s + RET_DK]
                out += [x1 * cos - x2 * sin, x1 * sin + x2 * cos]
            p = jnp.concatenate(out, axis=-1)
        elif mode == "del_conv":
            p = _silu(_short_conv(p, cw_ref[:, cols], trow, ROW_BLOCK))
            if c0 < DEL_QK:
                p = _l2norm_heads(p, DEL_DK, DEL_DK ** -0.5)
            elif c0 < 2 * DEL_QK:
                p = _l2norm_heads(p, DEL_DK, 1.0)
        o_ref[:, cols] = p.astype(BF16)


def _inproj(x, cidx, mods, norm_w3, w, layer, j, mode, *, wab=None, convw=None, rope=None, blocks_per_seq=1):
    n = x.shape[0]
    ncols = w.shape[2]
    in_specs = [
        pl.BlockSpec((ROW_BLOCK, D_MODEL), lambda i, c: (i, 0)),
        pl.BlockSpec((None, 8, 3 * D_MODEL), lambda i, c: (layer, 0, 0)),
        pl.BlockSpec((None, 1, D_MODEL), lambda i, c: (layer, 0, 0)),
        pl.BlockSpec((None, D_MODEL, ncols), lambda i, c: (j, 0, 0)),
    ]
    out_specs = [pl.BlockSpec((ROW_BLOCK, RET_IN), lambda i, c: (i, 0))]
    out_shape = [jax.ShapeDtypeStruct((n, RET_IN), BF16)]
    args = [x, mods, norm_w3, w]
    if mode == "ret_rope":
        cos, sin = rope
        spec = pl.BlockSpec((ROW_BLOCK, LANES), lambda i, c: (i % blocks_per_seq, 0))
        in_specs += [spec, spec]
        args += [cos, sin]
    if mode.startswith("del"):
        in_specs.append(pl.BlockSpec((None, D_MODEL, LANES), lambda i, c: (j, 0, 0)))
        args.append(wab)
        if mode == "del_conv":
            in_specs.append(pl.BlockSpec((None, CONV_W, DEL_CONV), lambda i, c: (j, 0, 0)))
            args.append(convw)
        out_specs.append(pl.BlockSpec((ROW_BLOCK, LANES), lambda i, c: (i, 0)))
        out_shape.append(jax.ShapeDtypeStruct((n, LANES), F32))
    res = pl.pallas_call(
        functools.partial(_inproj_body, mode=mode),
        grid_spec=pltpu.PrefetchScalarGridSpec(
            num_scalar_prefetch=1, grid=(n // ROW_BLOCK,), in_specs=in_specs, out_specs=out_specs),
        out_shape=out_shape,
        compiler_params=pltpu.CompilerParams(vmem_limit_bytes=VMEM_LIMIT),
        name="inproj_" + mode,
    )(cidx, *args)
    return res if mode.startswith("del") else res[0]


def _gated_residual(x_ref, mod_ref, cond_row, wout_ref, oall_s, fnw_ref, y_ref):
    ng, _, width = oall_s.shape
    acc = _dot(oall_s[0], wout_ref[0:width, :])
    for g in range(1, ng):
        acc = acc + _dot(oall_s[g], wout_ref[g * width:(g + 1) * width, :])
    gate = mod_ref[pl.ds(cond_row, 1), :][:, 2 * D_MODEL:]
    xn = x_ref[...] + gate * acc
    if fnw_ref is not None:
        ms = jnp.mean(xn * xn, axis=-1, keepdims=True)
        xn = xn * lax.rsqrt(ms + EPS) * fnw_ref[...]
    y_ref[...] = xn


def _residual_specs(T, vdim, layer, j, final):
    specs = [
        pl.BlockSpec((T, D_MODEL), lambda b, h: (b, 0)),
        pl.BlockSpec((None, 8, 3 * D_MODEL), lambda b, h: (layer, 0, 0)),
        pl.BlockSpec((None, vdim, D_MODEL), lambda b, h: (j, 0, 0)),
    ]
    if final:
        specs.append(pl.BlockSpec((1, D_MODEL), lambda b, h: (0, 0)))
    return specs


def _ret_body(decay_ref, q_ref, k_ref, v_ref, z_ref, gnw_ref, x_ref, mod_ref, wout_ref, *rest, T, latent, final,
              cache_masks):
    fnw_ref = None
    if final:
        fnw_ref, rest = rest[0], rest[1:]
    if latent:
        s0_ref, y_ref, m_s, oall_s = rest
    else:
        y_ref, st_ref, m_s, oall_s = rest
    b = pl.program_id(0)
    hd = pl.program_id(1)
    lgf = _log_sigmoid(jnp.full((1, 1), decay_ref[0, hd], F32))
    lgb = _log_sigmoid(jnp.full((1, 1), decay_ref[1, hd], F32))
    scale = RET_DK ** -0.5
    QT = 256
    mi = hd if cache_masks else 0

    def _build_mask():
        for qi in range(T // QT):
            ti = lax.broadcasted_iota(jnp.int32, (QT, T), 0) + qi * QT
            tj = lax.broadcasted_iota(jnp.int32, (QT, T), 1)
            dd = (ti - tj).astype(F32)
            lg = jnp.where(dd > 0, lgf, lgb)
            m_s[mi, qi * QT:(qi + 1) * QT, :] = jnp.where(dd == 0, 2.0 * scale, jnp.exp(lg * jnp.abs(dd)) * scale)

    if cache_masks:
        pl.when(b == 0)(_build_mask)
    else:
        _build_mask()

    q = q_ref[...]
    k = k_ref[...]
    v = v_ref[...]
    tcol = lax.broadcasted_iota(jnp.int32, (T, 1), 0).astype(F32)
    if latent:
        s0f = s0_ref[0].astype(BF16)
        s0b = s0_ref[1].astype(BF16)

    gnw = gnw_ref[...]
    for qi in range(T // QT):
        rows = slice(qi * QT, (qi + 1) * QT)
        qt = q[rows]
        s = _dot_nt(qt, k) * m_s[mi, rows, :]
        o = _dot(s.astype(BF16), v)
        if latent:
            tq = tcol[rows]
            qf = qt.astype(F32)
            xif = jnp.exp(lgf * (tq + 1.0))
            xib = jnp.exp(lgb * (float(T) - tq))
            o = o + _dot((qf * xif).astype(BF16), s0f) + _dot((qf * xib).astype(BF16), s0b)
        mu = jnp.mean(o, axis=-1, keepdims=True)
        oc = o - mu
        var = jnp.mean(oc * oc, axis=-1, keepdims=True)
        on = oc * lax.rsqrt(var + EPS) * gnw
        oall_s[hd, rows, :] = (on * z_ref[rows, :].astype(F32)).astype(BF16)

    if not latent:
        kf = k.astype(F32)
        zf = jnp.exp(lgf * (float(T - 1) - tcol)) * scale
        zb = jnp.exp(lgb * tcol) * scale
        st_ref[0] = _dot_tn((kf * zf).astype(BF16), v)
        st_ref[1] = _dot_tn((kf * zb).astype(BF16), v)

    @pl.when(hd == RET_HEADS - 1)
    def _finish_sequence():
        _gated_residual(x_ref, mod_ref, 1 + b if latent else 0, wout_ref, oall_s, fnw_ref, y_ref)


def _ret_mix(x, proj, decay, gnw, mods, w_out, T, *, layer, j, latent, fnw=None, s0=None, state_in=None):
    n = proj.shape[0]
    nseq = n // T
    final = fnw is not None
    cache_masks = not latent
    smem = pl.BlockSpec(memory_space=pltpu.SMEM)
    in_specs = [
        smem,
        pl.BlockSpec((T, RET_DK), lambda b, h: (b, h)),
        pl.BlockSpec((T, RET_DK), lambda b, h: (b, RET_HEADS + h)),
        pl.BlockSpec((T, RET_DV), lambda b, h: (b, RET_HEADS + h)),
        pl.BlockSpec((T, RET_DV), lambda b, h: (b, 2 * RET_HEADS + h)),
        pl.BlockSpec((1, RET_DV), lambda b, h: (0, h)),
    ] + _residual_specs(T, RET_V, layer, j, final)
    args = [decay, proj, proj, proj, proj, gnw, x, mods, w_out] + ([fnw] if final else [])
    y_spec = pl.BlockSpec((T, D_MODEL), lambda b, h: (b, 0))
    y_shape = jax.ShapeDtypeStruct((n, D_MODEL), F32)
    aliases = {}
    if latent:
        in_specs.append(pl.BlockSpec((None, None, 2, None, RET_DK, RET_DV), lambda b, h: (b, j, 0, h, 0, 0)))
        args.append(s0)
        out_specs = y_spec
        out_shape = y_shape
    else:
        st_spec = pl.BlockSpec((None, None, 2, None, RET_DK, RET_DV), lambda b, h: (b, j, 0, h, 0, 0))
        st_shape = jax.ShapeDtypeStruct((nseq, 2, 2, RET_HEADS, RET_DK, RET_DV), F32)
        out_specs = [y_spec, st_spec]
        out_shape = [y_shape, st_shape]
        if state_in is not None:
            in_specs.append(pl.BlockSpec(memory_space=pl.ANY))
            args.append(state_in)
            aliases = {len(args) - 1: 1}
    body = functools.partial(_ret_body, T=T, latent=latent, final=final, cache_masks=cache_masks)
    if state_in is not None:
        inner = body

        def body(*refs):
            n_in = len(args)
            return inner(*refs[:n_in - 1], *refs[n_in:])

    return pl.pallas_call(
        body,
        grid=(nseq, RET_HEADS),
        in_specs=in_specs,
        out_specs=out_specs,
        out_shape=out_shape,
        scratch_shapes=[
            pltpu.VMEM((RET_HEADS if cache_masks else 1, T, T), F32),
            pltpu.VMEM((RET_HEADS, T, RET_DV), BF16),
        ],
        input_output_aliases=aliases,
        compiler_params=pltpu.CompilerParams(vmem_limit_bytes=VMEM_LIMIT),
        name="ret_mix_lat" if latent else "ret_mix_ctx",
    )(*args)


def _seg_cumsum(x, pos, axis, reverse):
    n = x.shape[axis]
    s = 1
    while s < SUPER:
        if reverse:
            x = x + jnp.where(pos < SUPER - s, pltpu.roll(x, n - s, axis), 0.0)
        else:
            x = x + jnp.where(pos >= s, pltpu.roll(x, s, axis), 0.0)
        s *= 2
    return x


def _sibling_rows(x, b, sib):
    return jnp.concatenate([x[(2 * m + sib) * b:(2 * m + sib + 1) * b] for m in range(x.shape[0] // (2 * b))], axis=0)


def _tri_inverses(Abs, dirs, eye_s, lm_s, lmh_s):
    n = len(Abs)
    Xs = [eye_s[...] - Abs[c] * lm_s[dirs[c], 0] for c in range(n)]
    for p in range(1, N_LEVELS):
        b = 1 << p
        if b < BF16_ROWS:
            Ps = [_dot(Xs[c], Abs[c]) for c in range(n)]
            Pm = [Ps[c].astype(BF16) * lm_s[dirs[c], p] for c in range(n)]
            Us = [_dot(Pm[c], Xs[c]) for c in range(n)]
            Xs = [Xs[c] - Us[c].astype(BF16) for c in range(n)]
        else:
            Ps = [_dot(_sibling_rows(Xs[c], b, 1 - dirs[c]), Abs[c]) for c in range(n)]
            Pm = [Ps[c].astype(BF16) * lmh_s[dirs[c], p] for c in range(n)]
            Us = [_dot(Pm[c], Xs[c]).astype(BF16) for c in range(n)]
            for c in range(n):
                d = dirs[c]
                sib = 1 - d
                pieces = []
                for m in range(SUPER // (2 * b)):
                    keep = Xs[c][(2 * m + 1 - sib) * b:(2 * m + 2 - sib) * b]
                    new = Xs[c][(2 * m + sib) * b:(2 * m + sib + 1) * b] - Us[c][m * b:(m + 1) * b]
                    pieces += [new, keep] if d else [keep, new]
                Xs[c] = jnp.concatenate(pieces, axis=0)
    return Xs


def _delta_body(alog_ref, dtb_ref, q_ref, k_ref, v_ref, z_ref, ab_ref, nw_ref, x_ref, mod_ref, wout_ref, *rest,
                T, latent, preconv, final, hp):
    fnw_ref = None
    if final:
        fnw_ref, rest = rest[0], rest[1:]
    if not preconv:
        cwq_ref, cwk_ref, cwv_ref = rest[:3]
        rest = rest[3:]
    if latent:
        s0_ref, y_ref, eye_s, lm_s, lmh_s, tri_s, gt_s, oacc_s, oall_s = rest
    else:
        y_ref, st_ref, eye_s, lm_s, lmh_s, tri_s, gt_s, oacc_s, oall_s = rest
    nsup = T // SUPER

    @pl.when((pl.program_id(0) == 0) & (pl.program_id(1) == 0))
    def _init_masks():
        ii = lax.broadcasted_iota(jnp.int32, (SUPER, SUPER), 0)
        jj = lax.broadcasted_iota(jnp.int32, (SUPER, SUPER), 1)
        eye_s[...] = jnp.where(ii == jj, 1.0, 0.0).astype(BF16)
        xor = ii ^ jj
        lvl = jnp.zeros((SUPER, SUPER), jnp.int32)
        for p in range(1, N_LEVELS):
            lvl = lvl + (xor >= (1 << p)).astype(jnp.int32)
        for d in range(2):
            strict = (ii < jj) if d else (ii > jj)
            for p in range(N_LEVELS):
                m = jnp.where(strict & (lvl == p), 1.0, 0.0).astype(BF16)
                lm_s[d, p] = m
                if (1 << p) >= BF16_ROWS:
                    lmh_s[d, p] = _sibling_rows(m, 1 << p, 1 - d)
            tri_s[d] = jnp.where((ii <= jj) if d else (ii >= jj), 1.0, 0.0).astype(BF16)

    trow = lax.broadcasted_iota(jnp.int32, (T, 1), 0)
    pos_c = trow & (SUPER - 1)
    lane = lax.broadcasted_iota(jnp.int32, (T, LANES), 1)

    ab = ab_ref[...]
    g_all = -jnp.exp(alog_ref[...]) * _softplus(ab + dtb_ref[...])
    beta_all = _sigmoid(ab)
    pre = _seg_cumsum(g_all, pos_c, 0, False)
    suf = _seg_cumsum(g_all, pos_c, 0, True)
    G_all = jnp.where(lane >= DEL_HEADS, suf, pre)
    tot_all = pre + suf - g_all
    gt_s[...] = G_all.T

    def column(x, col):
        return jnp.sum(jnp.where(lane == col, x, 0.0), axis=1, keepdims=True)

    probs = []
    for hh in range(hp):
        head = pl.program_id(1) * hp + hh
        cq = slice(hh * DEL_DK, (hh + 1) * DEL_DK)
        cv = slice(hh * DEL_DV, (hh + 1) * DEL_DV)
        if preconv:
            qb = q_ref[:, cq]
            kb = k_ref[:, cq]
            q = qb.astype(F32)
            k = kb.astype(F32)
            v = v_ref[:, cv].astype(F32)
        else:
            q = _silu(_short_conv(q_ref[:, cq].astype(F32), cwq_ref[:, cq], trow, T))
            k = _silu(_short_conv(k_ref[:, cq].astype(F32), cwk_ref[:, cq], trow, T))
            v = _silu(_short_conv(v_ref[:, cv].astype(F32), cwv_ref[:, cv], trow, T))
            q = _l2norm_heads(q, DEL_DK, DEL_DK ** -0.5)
            k = _l2norm_heads(k, DEL_DK, 1.0)
            qb = q.astype(BF16)
            kb = k.astype(BF16)
        sup = [slice(r * SUPER, (r + 1) * SUPER) for r in range(nsup)]
        kks = [_dot_nt(kb[rows], kb[rows]) for rows in sup]
        qks = [_dot_nt(qb[rows], kb[rows]) for rows in sup]
        for d in range(2):
            col = d * DEL_HEADS + head
            beta = column(beta_all, 2 * DEL_HEADS + col)
            G = column(G_all, col)
            tot = column(tot_all, col)
            G_row = gt_s[pl.ds(col, 1), :]
            for r, rows in enumerate(sup):
                bt = beta[rows]
                Gc = G[rows]
                E = jnp.exp(jnp.minimum(Gc - G_row[:, rows], 0.0))
                pr = dict(hh=hh, d=d, r=r, rows=rows, cv=cv,
                          Ab=(bt * kks[r] * E).astype(BF16),
                          qkl=(qks[r] * E).astype(BF16) * tri_s[d],
                          vb=(v[rows] * bt).astype(BF16),
                          ke=(k[rows] * jnp.exp(tot[rows] - Gc)).astype(BF16))
                if latent:
                    eG = jnp.exp(Gc)
                    pr.update(kw=(k[rows] * (bt * eG)).astype(BF16), qe=q[rows] * eG,
                              et=jnp.exp(tot[r * SUPER:r * SUPER + 1]))
                probs.append(pr)

    Xs = _tri_inverses([pr["Ab"] for pr in probs], [pr["d"] for pr in probs], eye_s, lm_s, lmh_s)
    ubs = [_dot(X, pr["vb"]).astype(BF16) for X, pr in zip(Xs, probs)]
    o0s = [_dot(pr["qkl"], ub) for pr, ub in zip(probs, ubs)]
    Rs = [_dot_tn(pr["ke"], ub) for pr, ub in zip(probs, ubs)]
    for pr, o0 in zip(probs, o0s):
        if pr["d"]:
            oacc_s[pr["rows"], pr["cv"]] = oacc_s[pr["rows"], pr["cv"]] + o0
        else:
            oacc_s[pr["rows"], pr["cv"]] = o0

    if latent:
        wbs = [_dot(X, pr["kw"]).astype(BF16) for X, pr in zip(Xs, probs)]
        qts = [(pr["qe"] - _dot(pr["qkl"], wb)).astype(BF16) for pr, wb in zip(probs, wbs)]
        kms = [_dot_tn(pr["ke"], wb).astype(BF16) for pr, wb in zip(probs, wbs)]
        by_key = {(pr["hh"], pr["d"], pr["r"]): c for c, pr in enumerate(probs)}
        states = {(hh, d): s0_ref[d, hh] for hh in range(hp) for d in range(2)}
        for step in range(nsup):
            for hh in range(hp):
                for d in range(2):
                    c = by_key[(hh, d, nsup - 1 - step if d else step)]
                    pr = probs[c]
                    S = states[(hh, d)]
                    Sb = S.astype(BF16)
                    oacc_s[pr["rows"], pr["cv"]] = oacc_s[pr["rows"], pr["cv"]] + _dot(qts[c], Sb)
                    states[(hh, d)] = S * pr["et"] + Rs[c] - _dot(kms[c], Sb)
    else:
        for pr, R in zip(probs, Rs):
            st_ref[pr["d"], pr["hh"]] = R

    for hh in range(hp):
        cv = slice(hh * DEL_DV, (hh + 1) * DEL_DV)
        o = oacc_s[:, cv]
        o = o * lax.rsqrt(jnp.mean(o * o, axis=-1, keepdims=True) + EPS)
        oall_s[pl.program_id(1), :, cv] = (o * nw_ref[:, cv] * z_ref[:, cv].astype(F32)).astype(BF16)

    @pl.when(pl.program_id(1) == pl.num_programs(1) - 1)
    def _finish_sequence():
        _gated_residual(x_ref, mod_ref, 1 + pl.program_id(0) if latent else 0, wout_ref, oall_s, fnw_ref, y_ref)


def _delta_mix(x, proj, ab, alog, dtb, convw, nw, mods, w_out, T, *, layer, j, hp, latent, preconv, fnw=None,
               s0=None, state_in=None):
    n = proj.shape[0]
    nseq = n // T
    final = fnw is not None
    H = DEL_HEADS
    ng = H // hp
    wk, wv = hp * DEL_DK, hp * DEL_DV
    alog = jnp.pad(alog.reshape(1, 2 * H), ((0, 0), (0, LANES - 2 * H)))
    dtb = jnp.pad(dtb.reshape(1, 2 * H), ((0, 0), (0, LANES - 2 * H)))
    row = pl.BlockSpec((1, LANES), lambda b, h: (0, 0))
    in_specs = [
        row, row,
        pl.BlockSpec((T, wk), lambda b, h: (b, h)),
        pl.BlockSpec((T, wk), lambda b, h: (b, ng + h)),
        pl.BlockSpec((T, wv), lambda b, h: (b, ng + h)),
        pl.BlockSpec((T, wv), lambda b, h: (b, 2 * ng + h)),
        pl.BlockSpec((T, LANES), lambda b, h: (b, 0)),
        pl.BlockSpec((1, wv), lambda b, h: (0, h)),
    ] + _residual_specs(T, DEL_V, layer, j, final)
    args = [alog, dtb, proj, proj, proj, proj, ab, nw, x, mods, w_out] + ([fnw] if final else [])
    if not preconv:
        in_specs += [
            pl.BlockSpec((None, CONV_W, wk), lambda b, h: (j, 0, h)),
            pl.BlockSpec((None, CONV_W, wk), lambda b, h: (j, 0, ng + h)),
            pl.BlockSpec((None, CONV_W, wv), lambda b, h: (j, 0, ng + h)),
        ]
        args += [convw, convw, convw]
    y_spec = pl.BlockSpec((T, D_MODEL), lambda b, h: (b, 0))
    y_shape = jax.ShapeDtypeStruct((n, D_MODEL), F32)
    st_block = (None, None, 2, hp, DEL_DK, DEL_DV)
    aliases = {}
    scratch = [
        pltpu.VMEM((SUPER, SUPER), BF16),
        pltpu.VMEM((2, N_LEVELS, SUPER, SUPER), BF16),
        pltpu.VMEM((2, N_LEVELS, SUPER // 2, SUPER), BF16),
        pltpu.VMEM((2, SUPER, SUPER), BF16),
        pltpu.VMEM((LANES, T), F32),
        pltpu.VMEM((T, wv), F32),
        pltpu.VMEM((ng, T, wv), BF16),
    ]
    if latent:
        in_specs.append(pl.BlockSpec(st_block, lambda b, h: (b, j, 0, h, 0, 0)))
        args.append(s0)
        out_specs = y_spec
        out_shape = y_shape
    else:
        st_spec = pl.BlockSpec(st_block, lambda b, h: (b, j, 0, h, 0, 0))
        st_shape = jax.ShapeDtypeStruct((nseq, 2, 2, H, DEL_DK, DEL_DV), F32)
        out_specs = [y_spec, st_spec]
        out_shape = [y_shape, st_shape]
        if state_in is not None:
            in_specs.append(pl.BlockSpec(memory_space=pl.ANY))
            args.append(state_in)
            aliases = {len(args) - 1: 1}
    body = functools.partial(_delta_body, T=T, latent=latent, preconv=preconv, final=final, hp=hp)
    if state_in is not None:
        inner = body

        def body(*refs):
            n_in = len(args)
            return inner(*refs[:n_in - 1], *refs[n_in:])

    return pl.pallas_call(
        body,
        grid=(nseq, ng),
        in_specs=in_specs,
        out_specs=out_specs,
        out_shape=out_shape,
        scratch_shapes=scratch,
        input_output_aliases=aliases,
        compiler_params=pltpu.CompilerParams(vmem_limit_bytes=VMEM_LIMIT),
        name="delta_mix_lat" if latent else "delta_mix_ctx",
    )(*args)


def _rope_tables(T):
    rows = T // GRID_W
    r = jnp.broadcast_to(jnp.arange(rows)[:, None], (rows, GRID_W)).reshape(T).astype(F32)
    col = jnp.broadcast_to(jnp.arange(GRID_W)[None, :], (rows, GRID_W)).reshape(T).astype(F32)
    n_pairs = RET_DK // 4
    freqs = ROPE_BASE ** (-jnp.arange(n_pairs, dtype=F32) / n_pairs)
    ang = jnp.concatenate([r[:, None] * freqs, col[:, None] * freqs], -1)
    return jnp.cos(ang), jnp.sin(ang)


def _prepare_params(norm_w, ret_w_in, ret_w_out, del_w_in, del_w_out, final_norm_w):
    return dict(
        norm_w3=norm_w[:, None, :],
        ret_w_in=ret_w_in.astype(BF16),
        ret_w_out=ret_w_out.astype(BF16),
        del_w_in=del_w_in[:, :, :DEL_MAIN].astype(BF16),
        del_w_ab=jnp.pad(del_w_in[:, :, DEL_MAIN:], ((0, 0), (0, 0), (0, LANES - DEL_AB))).astype(BF16),
        del_w_out=del_w_out.astype(BF16),
        fnw=final_norm_w[None, :],
    )


def _group_trunk(x, cidx, T, latent, mods, prm, ret_decay, ret_gn_w, del_conv_w, del_a_log, del_dt_bias, del_norm_w,
                 state_ret=None, state_delta=None, depth=DEPTH):
    rope = _rope_tables(T) if latent else None
    preconv = (not latent) and T == ROW_BLOCK
    st_ret = None
    st_del = None
    for i in range(depth):
        j = i // 2
        fnw = prm["fnw"] if i == depth - 1 else None
        if i % 2 == 0:
            proj = _inproj(x, cidx, mods, prm["norm_w3"], prm["ret_w_in"], i, j, "ret_rope" if latent else "ret",
                           rope=rope, blocks_per_seq=T // ROW_BLOCK)
            gnw = ret_gn_w[j][None, :]
            common = dict(layer=i, j=j, latent=latent, fnw=fnw)
            if latent:
                x = _ret_mix(x, proj, ret_decay[j], gnw, mods, prm["ret_w_out"], T, s0=state_ret, **common)
            else:
                x, st_ret = _ret_mix(x, proj, ret_decay[j], gnw, mods, prm["ret_w_out"], T, state_in=st_ret, **common)
        else:
            proj, ab = _inproj(x, cidx, mods, prm["norm_w3"], prm["del_w_in"], i, j,
                               "del_conv" if preconv else "del_raw", wab=prm["del_w_ab"], convw=del_conv_w)
            dnw = del_norm_w[j][None, :]
            common = dict(layer=i, j=j, latent=latent, preconv=preconv, fnw=fnw)
            if latent:
                x = _delta_mix(x, proj, ab, del_a_log[j], del_dt_bias[j], del_conv_w, dnw, mods, prm["del_w_out"], T,
                               hp=1, s0=state_delta, **common)
            else:
                x, st_del = _delta_mix(x, proj, ab, del_a_log[j], del_dt_bias[j], del_conv_w, dnw, mods,
                                       prm["del_w_out"], T, hp=4, state_in=st_del, **common)
    return x, st_ret, st_del


def kernel(x_prompt, x_sample, state_ret, state_delta, c, c_ctx, norm_w, mod_w, mod_b, ret_w_in, ret_decay,
           ret_gn_w, ret_w_out, del_w_in, del_conv_w, del_a_log, del_dt_bias, del_norm_w, del_w_out,
           final_norm_w):
    B, T_ctx, _ = x_prompt.shape
    Bd, T_lat, _ = x_sample.shape
    n_ctx = B * T_ctx
    n_lat = Bd * T_lat

    cond8 = jnp.zeros((8, D_MODEL), F32).at[0].set(c_ctx).at[1:1 + Bd].set(c)
    mods = _modulation(cond8, mod_w, mod_b)
    cidx_ctx = jnp.zeros((n_ctx // ROW_BLOCK,), jnp.int32)
    cidx_lat = 1 + jnp.arange(n_lat // ROW_BLOCK, dtype=jnp.int32) // (T_lat // ROW_BLOCK)
    prm = _prepare_params(norm_w, ret_w_in, ret_w_out, del_w_in, del_w_out, final_norm_w)
    shared = (mods, prm, ret_decay, ret_gn_w, del_conv_w, del_a_log, del_dt_bias, del_norm_w)

    y_ctx, st_ret, st_del = _group_trunk(x_prompt.reshape(n_ctx, D_MODEL), cidx_ctx, T_ctx, False, *shared)
    y_lat, _, _ = _group_trunk(x_sample.reshape(n_lat, D_MODEL), cidx_lat, T_lat, True, *shared,
                               state_ret=state_ret, state_delta=state_delta)
    return (y_ctx.reshape(B, T_ctx, D_MODEL), y_lat.reshape(Bd, T_lat, D_MODEL), st_ret, st_del)
```

```python
import functools

import jax
import jax.numpy as jnp
from jax import lax
from jax.experimental import pallas as pl
from jax.experimental.pallas import tpu as pltpu

F32 = jnp.float32
BF16 = jnp.bfloat16

D_MODEL = 1024
DEPTH = 4
GRID_W = 64
EPS = 1e-6
ROPE_BASE = 10000.0
CONV_W = 3

RET_HEADS = 4
RET_DK = 256
RET_DV = 512
RET_QK = RET_HEADS * RET_DK
RET_V = RET_HEADS * RET_DV
RET_IN = 2 * RET_QK + 2 * RET_V

DEL_HEADS = 8
DEL_DK = 128
DEL_DV = 256
DEL_QK = DEL_HEADS * DEL_DK
DEL_V = DEL_HEADS * DEL_DV
DEL_CONV = 2 * DEL_QK + DEL_V
DEL_MAIN = DEL_CONV + DEL_V
DEL_AB = 4 * DEL_HEADS
assert RET_IN == DEL_MAIN

ROW_BLOCK = 256
COL_GROUP = 512
SUPER = 256
N_LEVELS = SUPER.bit_length() - 1
LANES = 128
SUBLANES = 8
NEG_LOG2_E = -1.4426950408889634
BF16_ROWS = 16
VMEM_LIMIT = 56 * 1024 * 1024


def _sigmoid(x):
    return 1.0 / (1.0 + jnp.exp2(x * NEG_LOG2_E))


def _silu(x):
    return x * _sigmoid(x)


def _softplus(x):
    return jnp.maximum(x, 0.0) + jnp.log1p(jnp.exp(-jnp.abs(x)))


def _log_sigmoid(x):
    return jnp.minimum(x, 0.0) - jnp.log1p(jnp.exp(-jnp.abs(x)))


def _dot(a, b):
    return jnp.dot(a, b, preferred_element_type=F32)


def _dot_nt(a, b):
    return lax.dot_general(a, b, (((1,), (1,)), ((), ())), preferred_element_type=F32)


def _dot_tn(a, b):
    return lax.dot_general(a, b, (((0,), (0,)), ((), ())), preferred_element_type=F32)


def _short_conv(x, w, T):
    y = w[0:1, :] * pltpu.roll(x, 1, 0) + w[1:2, :] * x + w[2:3, :] * pltpu.roll(x, T - 1, 0)
    r8 = lax.broadcasted_iota(jnp.int32, (SUBLANES, 1), 0)
    head = y[:SUBLANES] - jnp.where(r8 == 0, w[0:1, :] * x[T - 1:T, :], 0.0)
    tail = y[T - SUBLANES:] - jnp.where(r8 == SUBLANES - 1, w[2:3, :] * x[0:1, :], 0.0)
    return jnp.concatenate([head, y[SUBLANES:T - SUBLANES], tail], axis=0)


def _l2norm_heads(x, width, scale):
    out = []
    for s in range(0, x.shape[1], width):
        xs = x[:, s:s + width]
        out.append(xs * (lax.rsqrt(jnp.sum(xs * xs, axis=-1, keepdims=True) + EPS) * scale))
    return jnp.concatenate(out, axis=-1)


def _mod_body(cond_ref, w_ref, b_ref, o_ref):
    sc = _silu(cond_ref[...])
    o_ref[...] = jnp.dot(sc, w_ref[...], preferred_element_type=F32,
                         precision=lax.Precision.HIGHEST) + b_ref[...]


def _modulation(cond8, mod_w, mod_b):
    nb = 3
    return pl.pallas_call(
        _mod_body,
        grid=(DEPTH, nb),
        in_specs=[
            pl.BlockSpec((8, D_MODEL), lambda i, j: (0, 0)),
            pl.BlockSpec((None, D_MODEL, D_MODEL), lambda i, j: (i, 0, j)),
            pl.BlockSpec((None, 1, D_MODEL), lambda i, j: (i, 0, j)),
        ],
        out_specs=pl.BlockSpec((None, 8, D_MODEL), lambda i, j: (i, 0, j)),
        out_shape=jax.ShapeDtypeStruct((DEPTH, 8, 3 * D_MODEL), F32),
        compiler_params=pltpu.CompilerParams(vmem_limit_bytes=VMEM_LIMIT),
        name="modulation",
    )(cond8, mod_w, mod_b.reshape(DEPTH, 1, 3 * D_MODEL))


def _inproj_body(cidx_ref, x_ref, mod_ref, nw_ref, w_ref, *rest, mode):
    if mode == "ret":
        (o_ref,) = rest
    elif mode == "ret_rope":
        cos_ref, sin_ref, o_ref = rest
    elif mode == "del_conv":
        wab_ref, cw_ref, o_ref, ab_ref = rest
    else:
        wab_ref, o_ref, ab_ref = rest
    ci = cidx_ref[pl.program_id(0)]
    x = x_ref[...]
    ms = jnp.mean(x * x, axis=-1, keepdims=True)
    xn = x * lax.rsqrt(ms + EPS) * nw_ref[...]
    m = mod_ref[pl.ds(ci, 1), :]
    shift = m[:, :D_MODEL]
    scale = m[:, D_MODEL:2 * D_MODEL]
    h = (xn * (1.0 + scale) + shift).astype(BF16)
    if mode.startswith("del"):
        ab_ref[...] = _dot(h, wab_ref[...])
    gate_start = 2 * RET_QK + RET_V
    for c0 in range(0, RET_IN, COL_GROUP):
        cols = slice(c0, c0 + COL_GROUP)
        p = _dot(h, w_ref[:, cols])
        if c0 >= gate_start:
            p = _silu(p)
        elif mode == "ret_rope" and c0 < 2 * RET_QK:
            cos = cos_ref[...]
            sin = sin_ref[...]
            hk = RET_DK // 2
            out = []
            for s in range(0, COL_GROUP, RET_DK):
                x1, x2 = p[:, s:s + hk], p[:, s + hk:s + RET_DK]
                out += [x1 * cos - x2 * sin, x1 * sin + x2 * cos]
            p = jnp.concatenate(out, axis=-1)
        elif mode == "del_conv":
            p = _silu(_short_conv(p, cw_ref[:, cols], ROW_BLOCK))
            if c0 < DEL_QK:
                p = _l2norm_heads(p, DEL_DK, DEL_DK ** -0.5)
            elif c0 < 2 * DEL_QK:
                p = _l2norm_heads(p, DEL_DK, 1.0)
        o_ref[:, cols] = p.astype(BF16)


def _inproj(x, cidx, mods, norm_w3, w, layer, j, mode, *, wab=None, convw=None, rope=None, blocks_per_seq=1):
    n = x.shape[0]
    ncols = w.shape[2]
    in_specs = [
        pl.BlockSpec((ROW_BLOCK, D_MODEL), lambda i, c: (i, 0)),
        pl.BlockSpec((None, 8, 3 * D_MODEL), lambda i, c: (layer, 0, 0)),
        pl.BlockSpec((None, 1, D_MODEL), lambda i, c: (layer, 0, 0)),
        pl.BlockSpec((None, D_MODEL, ncols), lambda i, c: (j, 0, 0)),
    ]
    out_specs = [pl.BlockSpec((ROW_BLOCK, RET_IN), lambda i, c: (i, 0))]
    out_shape = [jax.ShapeDtypeStruct((n, RET_IN), BF16)]
    args = [x, mods, norm_w3, w]
    if mode == "ret_rope":
        cos, sin = rope
        spec = pl.BlockSpec((ROW_BLOCK, LANES), lambda i, c: (i % blocks_per_seq, 0))
        in_specs += [spec, spec]
        args += [cos, sin]
    if mode.startswith("del"):
        in_specs.append(pl.BlockSpec((None, D_MODEL, LANES), lambda i, c: (j, 0, 0)))
        args.append(wab)
        if mode == "del_conv":
            in_specs.append(pl.BlockSpec((None, CONV_W, DEL_CONV), lambda i, c: (j, 0, 0)))
            args.append(convw)
        out_specs.append(pl.BlockSpec((ROW_BLOCK, LANES), lambda i, c: (i, 0)))
        out_shape.append(jax.ShapeDtypeStruct((n, LANES), F32))
    res = pl.pallas_call(
        functools.partial(_inproj_body, mode=mode),
        grid_spec=pltpu.PrefetchScalarGridSpec(
            num_scalar_prefetch=1, grid=(n // ROW_BLOCK,), in_specs=in_specs, out_specs=out_specs),
        out_shape=out_shape,
        compiler_params=pltpu.CompilerParams(vmem_limit_bytes=VMEM_LIMIT),
        name="inproj_" + mode,
    )(cidx, *args)
    return res if mode.startswith("del") else res[0]


def _gated_residual(x_ref, mod_ref, cond_row, wout_ref, oall_s, fnw_ref, y_ref):
    ng, _, width = oall_s.shape
    acc = _dot(oall_s[0], wout_ref[0:width, :])
    for g in range(1, ng):
        acc = acc + _dot(oall_s[g], wout_ref[g * width:(g + 1) * width, :])
    gate = mod_ref[pl.ds(cond_row, 1), :][:, 2 * D_MODEL:]
    xn = x_ref[...] + gate * acc
    if fnw_ref is not None:
        ms = jnp.mean(xn * xn, axis=-1, keepdims=True)
        xn = xn * lax.rsqrt(ms + EPS) * fnw_ref[...]
    y_ref[...] = xn


def _residual_specs(T, vdim, layer, j, final):
    specs = [
        pl.BlockSpec((T, D_MODEL), lambda b, h: (b, 0)),
        pl.BlockSpec((None, 8, 3 * D_MODEL), lambda b, h: (layer, 0, 0)),
        pl.BlockSpec((None, vdim, D_MODEL), lambda b, h: (j, 0, 0)),
    ]
    if final:
        specs.append(pl.BlockSpec((1, D_MODEL), lambda b, h: (0, 0)))
    return specs


def _ret_body(decay_ref, q_ref, k_ref, v_ref, z_ref, gnw_ref, x_ref, mod_ref, wout_ref, *rest, T, latent, final,
              cache_masks, hp):
    fnw_ref = None
    if final:
        fnw_ref, rest = rest[0], rest[1:]
    if latent:
        s0_ref, y_ref, m_s, oall_s = rest
    else:
        y_ref, st_ref, m_s, oall_s = rest
    b = pl.program_id(0)
    scale = RET_DK ** -0.5
    QT = 256
    tcol = lax.broadcasted_iota(jnp.int32, (T, 1), 0).astype(F32)

    for hh in range(hp):
        hd = pl.program_id(1) * hp + hh
        cq = slice(hh * RET_DK, (hh + 1) * RET_DK)
        cv = slice(hh * RET_DV, (hh + 1) * RET_DV)
        lgf = _log_sigmoid(jnp.full((1, 1), decay_ref[0, hd], F32))
        lgb = _log_sigmoid(jnp.full((1, 1), decay_ref[1, hd], F32))
        mi = hd if cache_masks else 0

        def _build_mask(lgf=lgf, lgb=lgb, mi=mi):
            for qi in range(T // QT):
                ti = lax.broadcasted_iota(jnp.int32, (QT, T), 0) + qi * QT
                tj = lax.broadcasted_iota(jnp.int32, (QT, T), 1)
                dd = (ti - tj).astype(F32)
                lg = jnp.where(dd > 0, lgf, lgb)
                m_s[mi, qi * QT:(qi + 1) * QT, :] = jnp.where(dd == 0, 2.0 * scale,
                                                               jnp.exp(lg * jnp.abs(dd)) * scale)

        if cache_masks:
            pl.when(b == 0)(_build_mask)
        else:
            _build_mask()

        q = q_ref[:, cq]
        k = k_ref[:, cq]
        v = v_ref[:, cv]
        if latent:
            s0f = s0_ref[0, hh].astype(BF16)
            s0b = s0_ref[1, hh].astype(BF16)
        gnw = gnw_ref[:, cv]
        for qi in range(T // QT):
            rows = slice(qi * QT, (qi + 1) * QT)
            qt = q[rows]
            s = _dot_nt(qt, k) * m_s[mi, rows, :]
            o = _dot(s.astype(BF16), v)
            if latent:
                tq = tcol[rows]
                qf = qt.astype(F32)
                xif = jnp.exp(lgf * (tq + 1.0))
                xib = jnp.exp(lgb * (float(T) - tq))
                o = o + _dot((qf * xif).astype(BF16), s0f) + _dot((qf * xib).astype(BF16), s0b)
            mu = jnp.mean(o, axis=-1, keepdims=True)
            oc = o - mu
            var = jnp.mean(oc * oc, axis=-1, keepdims=True)
            on = oc * lax.rsqrt(var + EPS) * gnw
            oall_s[pl.program_id(1), rows, cv] = (on * z_ref[rows, cv].astype(F32)).astype(BF16)

        if not latent:
            kf = k.astype(F32)
            zf = jnp.exp(lgf * (float(T - 1) - tcol)) * scale
            zb = jnp.exp(lgb * tcol) * scale
            st_ref[0, hh] = _dot_tn((kf * zf).astype(BF16), v)
            st_ref[1, hh] = _dot_tn((kf * zb).astype(BF16), v)

    @pl.when(pl.program_id(1) == pl.num_programs(1) - 1)
    def _finish_sequence():
        _gated_residual(x_ref, mod_ref, 1 + b if latent else 0, wout_ref, oall_s, fnw_ref, y_ref)


def _ret_mix(x, proj, decay, gnw, mods, w_out, T, *, layer, j, latent, hp, fnw=None, s0=None, state_in=None):
    n = proj.shape[0]
    nseq = n // T
    final = fnw is not None
    cache_masks = not latent
    H = RET_HEADS
    ng = H // hp
    wk, wv = hp * RET_DK, hp * RET_DV
    smem = pl.BlockSpec(memory_space=pltpu.SMEM)
    in_specs = [
        smem,
        pl.BlockSpec((T, wk), lambda b, h: (b, h)),
        pl.BlockSpec((T, wk), lambda b, h: (b, ng + h)),
        pl.BlockSpec((T, wv), lambda b, h: (b, ng + h)),
        pl.BlockSpec((T, wv), lambda b, h: (b, 2 * ng + h)),
        pl.BlockSpec((1, wv), lambda b, h: (0, h)),
    ] + _residual_specs(T, RET_V, layer, j, final)
    args = [decay, proj, proj, proj, proj, gnw, x, mods, w_out] + ([fnw] if final else [])
    y_spec = pl.BlockSpec((T, D_MODEL), lambda b, h: (b, 0))
    y_shape = jax.ShapeDtypeStruct((n, D_MODEL), F32)
    st_block = (None, None, 2, hp, RET_DK, RET_DV)
    aliases = {}
    if latent:
        in_specs.append(pl.BlockSpec(st_block, lambda b, h: (b, j, 0, h, 0, 0)))
        args.append(s0)
        out_specs = y_spec
        out_shape = y_shape
    else:
        st_spec = pl.BlockSpec(st_block, lambda b, h: (b, j, 0, h, 0, 0))
        st_shape = jax.ShapeDtypeStruct((nseq, 2, 2, H, RET_DK, RET_DV), F32)
        out_specs = [y_spec, st_spec]
        out_shape = [y_shape, st_shape]
        if state_in is not None:
            in_specs.append(pl.BlockSpec(memory_space=pl.ANY))
            args.append(state_in)
            aliases = {len(args) - 1: 1}
    body = functools.partial(_ret_body, T=T, latent=latent, final=final, cache_masks=cache_masks, hp=hp)
    if state_in is not None:
        inner = body

        def body(*refs):
            n_in = len(args)
            return inner(*refs[:n_in - 1], *refs[n_in:])

    return pl.pallas_call(
        body,
        grid=(nseq, ng),
        in_specs=in_specs,
        out_specs=out_specs,
        out_shape=out_shape,
        scratch_shapes=[
            pltpu.VMEM((H if cache_masks else 1, T, T), F32),
            pltpu.VMEM((ng, T, wv), BF16),
        ],
        input_output_aliases=aliases,
        compiler_params=pltpu.CompilerParams(vmem_limit_bytes=VMEM_LIMIT),
        name="ret_mix_lat" if latent else "ret_mix_ctx",
    )(*args)


def _seg_cumsum(x, pos, axis, reverse):
    n = x.shape[axis]
    s = 1
    while s < SUPER:
        if reverse:
            x = x + jnp.where(pos < SUPER - s, pltpu.roll(x, n - s, axis), 0.0)
        else:
            x = x + jnp.where(pos >= s, pltpu.roll(x, s, axis), 0.0)
        s *= 2
    return x


def _sibling_rows(x, b, sib):
    return jnp.concatenate([x[(2 * m + sib) * b:(2 * m + sib + 1) * b] for m in range(x.shape[0] // (2 * b))], axis=0)


def _tri_inverses(Abs, dirs, eye_s, lm_s, lmh_s):
    n = len(Abs)
    Xs = [eye_s[...] - Abs[c] * lm_s[dirs[c], 0] for c in range(n)]
    for p in range(1, N_LEVELS):
        b = 1 << p
        if b < BF16_ROWS:
            Ps = [_dot(Xs[c], Abs[c]) for c in range(n)]
            Pm = [Ps[c].astype(BF16) * lm_s[dirs[c], p] for c in range(n)]
            Us = [_dot(Pm[c], Xs[c]) for c in range(n)]
            Xs = [Xs[c] - Us[c].astype(BF16) for c in range(n)]
        else:
            Ps = [_dot(_sibling_rows(Xs[c], b, 1 - dirs[c]), Abs[c]) for c in range(n)]
            Pm = [Ps[c].astype(BF16) * lmh_s[dirs[c], p] for c in range(n)]
            Us = [_dot(Pm[c], Xs[c]).astype(BF16) for c in range(n)]
            for c in range(n):
                d = dirs[c]
                sib = 1 - d
                pieces = []
                for m in range(SUPER // (2 * b)):
                    keep = Xs[c][(2 * m + 1 - sib) * b:(2 * m + 2 - sib) * b]
                    new = Xs[c][(2 * m + sib) * b:(2 * m + sib + 1) * b] - Us[c][m * b:(m + 1) * b]
                    pieces += [new, keep] if d else [keep, new]
                Xs[c] = jnp.concatenate(pieces, axis=0)
    return Xs


def _delta_body(alog_ref, dtb_ref, q_ref, k_ref, v_ref, z_ref, ab_ref, nw_ref, x_ref, mod_ref, wout_ref, *rest,
                T, latent, preconv, final, hp):
    fnw_ref = None
    if final:
        fnw_ref, rest = rest[0], rest[1:]
    if not preconv:
        cwq_ref, cwk_ref, cwv_ref = rest[:3]
        rest = rest[3:]
    if latent:
        s0_ref, y_ref, eye_s, lm_s, lmh_s, tri_s, gt_s, oacc_s, oall_s = rest
    else:
        y_ref, st_ref, eye_s, lm_s, lmh_s, tri_s, gt_s, oacc_s, oall_s = rest
    nsup = T // SUPER

    @pl.when((pl.program_id(0) == 0) & (pl.program_id(1) == 0))
    def _init_masks():
        ii = lax.broadcasted_iota(jnp.int32, (SUPER, SUPER), 0)
        jj = lax.broadcasted_iota(jnp.int32, (SUPER, SUPER), 1)
        eye_s[...] = jnp.where(ii == jj, 1.0, 0.0).astype(BF16)
        xor = ii ^ jj
        lvl = jnp.zeros((SUPER, SUPER), jnp.int32)
        for p in range(1, N_LEVELS):
            lvl = lvl + (xor >= (1 << p)).astype(jnp.int32)
        for d in range(2):
            strict = (ii < jj) if d else (ii > jj)
            for p in range(N_LEVELS):
                m = jnp.where(strict & (lvl == p), 1.0, 0.0).astype(BF16)
                lm_s[d, p] = m
                if (1 << p) >= BF16_ROWS:
                    lmh_s[d, p] = _sibling_rows(m, 1 << p, 1 - d)
            tri_s[d] = jnp.where((ii <= jj) if d else (ii >= jj), 1.0, 0.0).astype(BF16)

    trow = lax.broadcasted_iota(jnp.int32, (T, 1), 0)
    pos_c = trow & (SUPER - 1)
    lane = lax.broadcasted_iota(jnp.int32, (T, LANES), 1)

    ab = ab_ref[...]
    g_all = -jnp.exp(alog_ref[...]) * _softplus(ab + dtb_ref[...])
    beta_all = _sigmoid(ab)
    pre = _seg_cumsum(g_all, pos_c, 0, False)
    suf = _seg_cumsum(g_all, pos_c, 0, True)
    G_all = jnp.where(lane >= DEL_HEADS, suf, pre)
    tot_all = pre + suf - g_all
    gt_s[...] = G_all.T

    def column(x, col):
        return jnp.sum(jnp.where(lane == col, x, 0.0), axis=1, keepdims=True)

    probs = []
    for hh in range(hp):
        head = pl.program_id(1) * hp + hh
        cq = slice(hh * DEL_DK, (hh + 1) * DEL_DK)
        cv = slice(hh * DEL_DV, (hh + 1) * DEL_DV)
        if preconv:
            qb = q_ref[:, cq]
            kb = k_ref[:, cq]
            q = qb.astype(F32)
            k = kb.astype(F32)
            v = v_ref[:, cv].astype(F32)
        else:
            q = _silu(_short_conv(q_ref[:, cq].astype(F32), cwq_ref[:, cq], T))
            k = _silu(_short_conv(k_ref[:, cq].astype(F32), cwk_ref[:, cq], T))
            v = _silu(_short_conv(v_ref[:, cv].astype(F32), cwv_ref[:, cv], T))
            q = _l2norm_heads(q, DEL_DK, DEL_DK ** -0.5)
            k = _l2norm_heads(k, DEL_DK, 1.0)
            qb = q.astype(BF16)
            kb = k.astype(BF16)
        sup = [slice(r * SUPER, (r + 1) * SUPER) for r in range(nsup)]
        kks = [_dot_nt(kb[rows], kb[rows]) for rows in sup]
        qks = [_dot_nt(qb[rows], kb[rows]) for rows in sup]
        for d in range(2):
            col = d * DEL_HEADS + head
            beta = column(beta_all, 2 * DEL_HEADS + col)
            G = column(G_all, col)
            tot = column(tot_all, col)
            G_row = gt_s[pl.ds(col, 1), :]
            for r, rows in enumerate(sup):
                bt = beta[rows]
                Gc = G[rows]
                E = jnp.exp(jnp.minimum(Gc - G_row[:, rows], 0.0))
                pr = dict(hh=hh, d=d, r=r, rows=rows, cv=cv,
                          Ab=(bt * kks[r] * E).astype(BF16),
                          qkl=(qks[r] * E).astype(BF16) * tri_s[d],
                          vb=(v[rows] * bt).astype(BF16),
                          ke=(k[rows] * jnp.exp(tot[rows] - Gc)).astype(BF16))
                if latent:
                    eG = jnp.exp(Gc)
                    pr.update(kw=(k[rows] * (bt * eG)).astype(BF16), qe=q[rows] * eG,
                              et=jnp.exp(tot[r * SUPER:r * SUPER + 1]))
                probs.append(pr)

    Xs = _tri_inverses([pr["Ab"] for pr in probs], [pr["d"] for pr in probs], eye_s, lm_s, lmh_s)
    ubs = [_dot(X, pr["vb"]).astype(BF16) for X, pr in zip(Xs, probs)]
    o0s = [_dot(pr["qkl"], ub) for pr, ub in zip(probs, ubs)]
    Rs = [_dot_tn(pr["ke"], ub) for pr, ub in zip(probs, ubs)]
    for pr, o0 in zip(probs, o0s):
        if pr["d"]:
            oacc_s[pr["rows"], pr["cv"]] = oacc_s[pr["rows"], pr["cv"]] + o0
        else:
            oacc_s[pr["rows"], pr["cv"]] = o0

    if latent:
        wbs = [_dot(X, pr["kw"]).astype(BF16) for X, pr in zip(Xs, probs)]
        qts = [(pr["qe"] - _dot(pr["qkl"], wb)).astype(BF16) for pr, wb in zip(probs, wbs)]
        kms = [_dot_tn(pr["ke"], wb).astype(BF16) for pr, wb in zip(probs, wbs)]
        by_key = {(pr["hh"], pr["d"], pr["r"]): c for c, pr in enumerate(probs)}
        states = {(hh, d): s0_ref[d, hh] for hh in range(hp) for d in range(2)}
        for step in range(nsup):
            for hh in range(hp):
                for d in range(2):
                    c = by_key[(hh, d, nsup - 1 - step if d else step)]
                    pr = probs[c]
                    S = states[(hh, d)]
                    Sb = S.astype(BF16)
                    oacc_s[pr["rows"], pr["cv"]] = oacc_s[pr["rows"], pr["cv"]] + _dot(qts[c], Sb)
                    states[(hh, d)] = S * pr["et"] + Rs[c] - _dot(kms[c], Sb)
    else:
        for pr, R in zip(probs, Rs):
            st_ref[pr["d"], pr["hh"]] = R

    for hh in range(hp):
        cv = slice(hh * DEL_DV, (hh + 1) * DEL_DV)
        o = oacc_s[:, cv]
        o = o * lax.rsqrt(jnp.mean(o * o, axis=-1, keepdims=True) + EPS)
        oall_s[pl.program_id(1), :, cv] = (o * nw_ref[:, cv] * z_ref[:, cv].astype(F32)).astype(BF16)

    @pl.when(pl.program_id(1) == pl.num_programs(1) - 1)
    def _finish_sequence():
        _gated_residual(x_ref, mod_ref, 1 + pl.program_id(0) if latent else 0, wout_ref, oall_s, fnw_ref, y_ref)


def _delta_mix(x, proj, ab, alog, dtb, convw, nw, mods, w_out, T, *, layer, j, hp, latent, preconv, fnw=None,
               s0=None, state_in=None):
    n = proj.shape[0]
    nseq = n // T
    final = fnw is not None
    H = DEL_HEADS
    ng = H // hp
    wk, wv = hp * DEL_DK, hp * DEL_DV
    alog = jnp.pad(alog.reshape(1, 2 * H), ((0, 0), (0, LANES - 2 * H)))
    dtb = jnp.pad(dtb.reshape(1, 2 * H), ((0, 0), (0, LANES - 2 * H)))
    row = pl.BlockSpec((1, LANES), lambda b, h: (0, 0))
    in_specs = [
        row, row,
        pl.BlockSpec((T, wk), lambda b, h: (b, h)),
        pl.BlockSpec((T, wk), lambda b, h: (b, ng + h)),
        pl.BlockSpec((T, wv), lambda b, h: (b, ng + h)),
        pl.BlockSpec((T, wv), lambda b, h: (b, 2 * ng + h)),
        pl.BlockSpec((T, LANES), lambda b, h: (b, 0)),
        pl.BlockSpec((1, wv), lambda b, h: (0, h)),
    ] + _residual_specs(T, DEL_V, layer, j, final)
    args = [alog, dtb, proj, proj, proj, proj, ab, nw, x, mods, w_out] + ([fnw] if final else [])
    if not preconv:
        in_specs += [
            pl.BlockSpec((None, CONV_W, wk), lambda b, h: (j, 0, h)),
            pl.BlockSpec((None, CONV_W, wk), lambda b, h: (j, 0, ng + h)),
            pl.BlockSpec((None, CONV_W, wv), lambda b, h: (j, 0, ng + h)),
        ]
        args += [convw, convw, convw]
    y_spec = pl.BlockSpec((T, D_MODEL), lambda b, h: (b, 0))
    y_shape = jax.ShapeDtypeStruct((n, D_MODEL), F32)
    st_block = (None, None, 2, hp, DEL_DK, DEL_DV)
    aliases = {}
    scratch = [
        pltpu.VMEM((SUPER, SUPER), BF16),
        pltpu.VMEM((2, N_LEVELS, SUPER, SUPER), BF16),
        pltpu.VMEM((2, N_LEVELS, SUPER // 2, SUPER), BF16),
        pltpu.VMEM((2, SUPER, SUPER), BF16),
        pltpu.VMEM((LANES, T), F32),
        pltpu.VMEM((T, wv), F32),
        pltpu.VMEM((ng, T, wv), BF16),
    ]
    if latent:
        in_specs.append(pl.BlockSpec(st_block, lambda b, h: (b, j, 0, h, 0, 0)))
        args.append(s0)
        out_specs = y_spec
        out_shape = y_shape
    else:
        st_spec = pl.BlockSpec(st_block, lambda b, h: (b, j, 0, h, 0, 0))
        st_shape = jax.ShapeDtypeStruct((nseq, 2, 2, H, DEL_DK, DEL_DV), F32)
        out_specs = [y_spec, st_spec]
        out_shape = [y_shape, st_shape]
        if state_in is not None:
            in_specs.append(pl.BlockSpec(memory_space=pl.ANY))
            args.append(state_in)
            aliases = {len(args) - 1: 1}
    body = functools.partial(_delta_body, T=T, latent=latent, preconv=preconv, final=final, hp=hp)
    if state_in is not None:
        inner = body

        def body(*refs):
            n_in = len(args)
            return inner(*refs[:n_in - 1], *refs[n_in:])

    return pl.pallas_call(
        body,
        grid=(nseq, ng),
        in_specs=in_specs,
        out_specs=out_specs,
        out_shape=out_shape,
        scratch_shapes=scratch,
        input_output_aliases=aliases,
        compiler_params=pltpu.CompilerParams(vmem_limit_bytes=VMEM_LIMIT),
        name="delta_mix_lat" if latent else "delta_mix_ctx",
    )(*args)


def _rope_tables(T):
    rows = T // GRID_W
    r = jnp.broadcast_to(jnp.arange(rows)[:, None], (rows, GRID_W)).reshape(T).astype(F32)
    col = jnp.broadcast_to(jnp.arange(GRID_W)[None, :], (rows, GRID_W)).reshape(T).astype(F32)
    n_pairs = RET_DK // 4
    freqs = ROPE_BASE ** (-jnp.arange(n_pairs, dtype=F32) / n_pairs)
    ang = jnp.concatenate([r[:, None] * freqs, col[:, None] * freqs], -1)
    return jnp.cos(ang), jnp.sin(ang)


def _split_cast_body(w_ref, main_ref, ab_ref):
    w = w_ref[...]
    main_ref[...] = w[:, :DEL_MAIN].astype(BF16)
    ab_ref[...] = jnp.zeros(ab_ref.shape, BF16)
    ab_ref[:, :DEL_AB] = w[:, DEL_MAIN:].astype(BF16)


def _split_cast_del_w_in(w):
    nl = w.shape[0]
    rb = 128
    return pl.pallas_call(
        _split_cast_body,
        grid=(nl, D_MODEL // rb),
        in_specs=[pl.BlockSpec((None, rb, DEL_MAIN + DEL_AB), lambda l, r: (l, r, 0))],
        out_specs=[pl.BlockSpec((None, rb, DEL_MAIN), lambda l, r: (l, r, 0)),
                   pl.BlockSpec((None, rb, LANES), lambda l, r: (l, r, 0))],
        out_shape=[jax.ShapeDtypeStruct((nl, D_MODEL, DEL_MAIN), BF16),
                   jax.ShapeDtypeStruct((nl, D_MODEL, LANES), BF16)],
        compiler_params=pltpu.CompilerParams(vmem_limit_bytes=VMEM_LIMIT),
        name="split_cast",
    )(w)


def _prepare_params(norm_w, ret_w_in, ret_w_out, del_w_in, del_w_out, final_norm_w):
    del_w_main, del_w_ab = _split_cast_del_w_in(del_w_in)
    return dict(
        norm_w3=norm_w[:, None, :],
        ret_w_in=ret_w_in.astype(BF16),
        ret_w_out=ret_w_out.astype(BF16),
        del_w_in=del_w_main,
        del_w_ab=del_w_ab,
        del_w_out=del_w_out.astype(BF16),
        fnw=final_norm_w[None, :],
    )


def _group_trunk(x, cidx, T, latent, mods, prm, ret_decay, ret_gn_w, del_conv_w, del_a_log, del_dt_bias, del_norm_w,
                 state_ret=None, state_delta=None, depth=DEPTH):
    rope = _rope_tables(T) if latent else None
    preconv = (not latent) and T == ROW_BLOCK
    st_ret = None
    st_del = None
    for i in range(depth):
        j = i // 2
        fnw = prm["fnw"] if i == depth - 1 else None
        if i % 2 == 0:
            proj = _inproj(x, cidx, mods, prm["norm_w3"], prm["ret_w_in"], i, j, "ret_rope" if latent else "ret",
                           rope=rope, blocks_per_seq=T // ROW_BLOCK)
            gnw = ret_gn_w[j][None, :]
            common = dict(layer=i, j=j, latent=latent, fnw=fnw)
            if latent:
                x = _ret_mix(x, proj, ret_decay[j], gnw, mods, prm["ret_w_out"], T, hp=1, s0=state_ret, **common)
            else:
                x, st_ret = _ret_mix(x, proj, ret_decay[j], gnw, mods, prm["ret_w_out"], T, hp=RET_HEADS,
                                     state_in=st_ret, **common)
        else:
            proj, ab = _inproj(x, cidx, mods, prm["norm_w3"], prm["del_w_in"], i, j,
                               "del_conv" if preconv else "del_raw", wab=prm["del_w_ab"], convw=del_conv_w)
            dnw = del_norm_w[j][None, :]
            common = dict(layer=i, j=j, latent=latent, preconv=preconv, fnw=fnw)
            if latent:
                x = _delta_mix(x, proj, ab, del_a_log[j], del_dt_bias[j], del_conv_w, dnw, mods, prm["del_w_out"], T,
                               hp=1, s0=state_delta, **common)
            else:
                x, st_del = _delta_mix(x, proj, ab, del_a_log[j], del_dt_bias[j], del_conv_w, dnw, mods,
                                       prm["del_w_out"], T, hp=4, state_in=st_del, **common)
    return x, st_ret, st_del


def kernel(x_prompt, x_sample, state_ret, state_delta, c, c_ctx, norm_w, mod_w, mod_b, ret_w_in, ret_decay,
           ret_gn_w, ret_w_out, del_w_in, del_conv_w, del_a_log, del_dt_bias, del_norm_w, del_w_out,
           final_norm_w):
    B, T_ctx, _ = x_prompt.shape
    Bd, T_lat, _ = x_sample.shape
    n_ctx = B * T_ctx
    n_lat = Bd * T_lat

    cond8 = jnp.zeros((8, D_MODEL), F32).at[0].set(c_ctx).at[1:1 + Bd].set(c)
    mods = _modulation(cond8, mod_w, mod_b)
    cidx_ctx = jnp.zeros((n_ctx // ROW_BLOCK,), jnp.int32)
    cidx_lat = 1 + jnp.arange(n_lat // ROW_BLOCK, dtype=jnp.int32) // (T_lat // ROW_BLOCK)
    prm = _prepare_params(norm_w, ret_w_in, ret_w_out, del_w_in, del_w_out, final_norm_w)
    shared = (mods, prm, ret_decay, ret_gn_w, del_conv_w, del_a_log, del_dt_bias, del_norm_w)

    y_ctx, st_ret, st_del = _group_trunk(x_prompt.reshape(n_ctx, D_MODEL), cidx_ctx, T_ctx, False, *shared)
    y_lat, _, _ = _group_trunk(x_sample.reshape(n_lat, D_MODEL), cidx_lat, T_lat, True, *shared,
                               state_ret=state_ret, state_delta=state_delta)
    return (y_ctx.reshape(B, T_ctx, D_MODEL), y_lat.reshape(Bd, T_lat, D_MODEL), st_ret, st_del)
```

```python
import functools

import jax
import jax.numpy as jnp
from jax import lax
from jax.experimental import pallas as pl
from jax.experimental.pallas import tpu as pltpu

F32 = jnp.float32
BF16 = jnp.bfloat16

D_MODEL = 1024
DEPTH = 4
GRID_W = 64
EPS = 1e-6
ROPE_BASE = 10000.0
CONV_W = 3

RET_HEADS = 4
RET_DK = 256
RET_DV = 512
RET_QK = RET_HEADS * RET_DK
RET_V = RET_HEADS * RET_DV
RET_IN = 2 * RET_QK + 2 * RET_V

DEL_HEADS = 8
DEL_DK = 128
DEL_DV = 256
DEL_QK = DEL_HEADS * DEL_DK
DEL_V = DEL_HEADS * DEL_DV
DEL_CONV = 2 * DEL_QK + DEL_V
DEL_MAIN = DEL_CONV + DEL_V
DEL_AB = 4 * DEL_HEADS
assert RET_IN == DEL_MAIN

ROW_BLOCK = 256
COL_GROUP = 512
SUPER = 256
N_LEVELS = SUPER.bit_length() - 1
LANES = 128
SUBLANES = 8
NEG_LOG2_E = -1.4426950408889634
BF16_ROWS = 16
VMEM_LIMIT = 56 * 1024 * 1024


def _sigmoid(x):
    return 1.0 / (1.0 + jnp.exp2(x * NEG_LOG2_E))


def _silu(x):
    return x * _sigmoid(x)


def _softplus(x):
    return jnp.maximum(x, 0.0) + jnp.log1p(jnp.exp(-jnp.abs(x)))


def _log_sigmoid(x):
    return jnp.minimum(x, 0.0) - jnp.log1p(jnp.exp(-jnp.abs(x)))


def _dot(a, b):
    return jnp.dot(a, b, preferred_element_type=F32)


def _dot_nt(a, b):
    return lax.dot_general(a, b, (((1,), (1,)), ((), ())), preferred_element_type=F32)


def _dot_tn(a, b):
    return lax.dot_general(a, b, (((0,), (0,)), ((), ())), preferred_element_type=F32)


def _short_conv(x, w, T):
    y = w[0:1, :] * pltpu.roll(x, 1, 0) + w[1:2, :] * x + w[2:3, :] * pltpu.roll(x, T - 1, 0)
    r8 = lax.broadcasted_iota(jnp.int32, (SUBLANES, 1), 0)
    head = y[:SUBLANES] - jnp.where(r8 == 0, w[0:1, :] * x[T - 1:T, :], 0.0)
    tail = y[T - SUBLANES:] - jnp.where(r8 == SUBLANES - 1, w[2:3, :] * x[0:1, :], 0.0)
    return jnp.concatenate([head, y[SUBLANES:T - SUBLANES], tail], axis=0)


def _l2norm_heads(x, width, scale):
    out = []
    for s in range(0, x.shape[1], width):
        xs = x[:, s:s + width]
        out.append(xs * (lax.rsqrt(jnp.sum(xs * xs, axis=-1, keepdims=True) + EPS) * scale))
    return jnp.concatenate(out, axis=-1)


def _mod_body(cond_ref, w_ref, b_ref, o_ref):
    sc = _silu(cond_ref[...])
    o_ref[...] = jnp.dot(sc, w_ref[...], preferred_element_type=F32,
                         precision=lax.Precision.HIGHEST) + b_ref[...]


def _modulation(cond8, mod_w, mod_b):
    nb = 3
    return pl.pallas_call(
        _mod_body,
        grid=(DEPTH, nb),
        in_specs=[
            pl.BlockSpec((8, D_MODEL), lambda i, j: (0, 0)),
            pl.BlockSpec((None, D_MODEL, D_MODEL), lambda i, j: (i, 0, j)),
            pl.BlockSpec((None, 1, D_MODEL), lambda i, j: (i, 0, j)),
        ],
        out_specs=pl.BlockSpec((None, 8, D_MODEL), lambda i, j: (i, 0, j)),
        out_shape=jax.ShapeDtypeStruct((DEPTH, 8, 3 * D_MODEL), F32),
        compiler_params=pltpu.CompilerParams(vmem_limit_bytes=VMEM_LIMIT),
        name="modulation",
    )(cond8, mod_w, mod_b.reshape(DEPTH, 1, 3 * D_MODEL))


def _inproj_body(cidx_ref, x_ref, mod_ref, nw_ref, w_ref, *rest, mode):
    if mode == "ret":
        (o_ref,) = rest
    elif mode == "ret_rope":
        cos_ref, sin_ref, o_ref = rest
    elif mode == "del_conv":
        wab_ref, cw_ref, o_ref, ab_ref = rest
    else:
        wab_ref, o_ref, ab_ref = rest
    ci = cidx_ref[pl.program_id(0)]
    x = x_ref[...]
    ms = jnp.mean(x * x, axis=-1, keepdims=True)
    xn = x * lax.rsqrt(ms + EPS) * nw_ref[...]
    m = mod_ref[pl.ds(ci, 1), :]
    shift = m[:, :D_MODEL]
    scale = m[:, D_MODEL:2 * D_MODEL]
    h = (xn * (1.0 + scale) + shift).astype(BF16)
    if mode.startswith("del"):
        ab_ref[...] = _dot(h, wab_ref[...])
    gate_start = 2 * RET_QK + RET_V
    for c0 in range(0, RET_IN, COL_GROUP):
        cols = slice(c0, c0 + COL_GROUP)
        p = _dot(h, w_ref[:, cols])
        if c0 >= gate_start:
            p = _silu(p)
        elif mode == "ret_rope" and c0 < 2 * RET_QK:
            cos = cos_ref[...]
            sin = sin_ref[...]
            hk = RET_DK // 2
            out = []
            for s in range(0, COL_GROUP, RET_DK):
                x1, x2 = p[:, s:s + hk], p[:, s + hk:s + RET_DK]
                out += [x1 * cos - x2 * sin, x1 * sin + x2 * cos]
            p = jnp.concatenate(out, axis=-1)
        elif mode == "del_conv":
            p = _silu(_short_conv(p, cw_ref[:, cols], ROW_BLOCK))
            if c0 < DEL_QK:
                p = _l2norm_heads(p, DEL_DK, DEL_DK ** -0.5)
            elif c0 < 2 * DEL_QK:
                p = _l2norm_heads(p, DEL_DK, 1.0)
        o_ref[:, cols] = p.astype(BF16)


def _inproj(x, cidx, mods, norm_w3, w, layer, j, mode, *, wab=None, convw=None, rope=None, blocks_per_seq=1):
    n = x.shape[0]
    ncols = w.shape[2]
    in_specs = [
        pl.BlockSpec((ROW_BLOCK, D_MODEL), lambda i, c: (i, 0)),
        pl.BlockSpec((None, 8, 3 * D_MODEL), lambda i, c: (layer, 0, 0)),
        pl.BlockSpec((None, 1, D_MODEL), lambda i, c: (layer, 0, 0)),
        pl.BlockSpec((None, D_MODEL, ncols), lambda i, c: (j, 0, 0)),
    ]
    out_specs = [pl.BlockSpec((ROW_BLOCK, RET_IN), lambda i, c: (i, 0))]
    out_shape = [jax.ShapeDtypeStruct((n, RET_IN), BF16)]
    args = [x, mods, norm_w3, w]
    if mode == "ret_rope":
        cos, sin = rope
        spec = pl.BlockSpec((ROW_BLOCK, LANES), lambda i, c: (i % blocks_per_seq, 0))
        in_specs += [spec, spec]
        args += [cos, sin]
    if mode.startswith("del"):
        in_specs.append(pl.BlockSpec((None, D_MODEL, LANES), lambda i, c: (j, 0, 0)))
        args.append(wab)
        if mode == "del_conv":
            in_specs.append(pl.BlockSpec((None, CONV_W, DEL_CONV), lambda i, c: (j, 0, 0)))
            args.append(convw)
        out_specs.append(pl.BlockSpec((ROW_BLOCK, LANES), lambda i, c: (i, 0)))
        out_shape.append(jax.ShapeDtypeStruct((n, LANES), F32))
    res = pl.pallas_call(
        functools.partial(_inproj_body, mode=mode),
        grid_spec=pltpu.PrefetchScalarGridSpec(
            num_scalar_prefetch=1, grid=(n // ROW_BLOCK,), in_specs=in_specs, out_specs=out_specs),
        out_shape=out_shape,
        compiler_params=pltpu.CompilerParams(vmem_limit_bytes=VMEM_LIMIT),
        name="inproj_" + mode,
    )(cidx, *args)
    return res if mode.startswith("del") else res[0]


def _gated_residual(x_ref, mod_ref, cond_row, wout_ref, oall_s, fnw_ref, y_ref):
    ng, _, width = oall_s.shape
    acc = _dot(oall_s[0], wout_ref[0:width, :])
    for g in range(1, ng):
        acc = acc + _dot(oall_s[g], wout_ref[g * width:(g + 1) * width, :])
    gate = mod_ref[pl.ds(cond_row, 1), :][:, 2 * D_MODEL:]
    xn = x_ref[...] + gate * acc
    if fnw_ref is not None:
        ms = jnp.mean(xn * xn, axis=-1, keepdims=True)
        xn = xn * lax.rsqrt(ms + EPS) * fnw_ref[...]
    y_ref[...] = xn


def _residual_specs(T, vdim, layer, j, final):
    specs = [
        pl.BlockSpec((T, D_MODEL), lambda b, h: (b, 0)),
        pl.BlockSpec((None, 8, 3 * D_MODEL), lambda b, h: (layer, 0, 0)),
        pl.BlockSpec((None, vdim, D_MODEL), lambda b, h: (j, 0, 0)),
    ]
    if final:
        specs.append(pl.BlockSpec((1, D_MODEL), lambda b, h: (0, 0)))
    return specs


def _ret_body(decay_ref, q_ref, k_ref, v_ref, z_ref, gnw_ref, x_ref, mod_ref, wout_ref, *rest, T, latent, final,
              cache_masks, hp):
    fnw_ref = None
    if final:
        fnw_ref, rest = rest[0], rest[1:]
    if latent:
        s0_ref, y_ref, m_s, oall_s = rest
    else:
        y_ref, st_ref, m_s, oall_s = rest
    b = pl.program_id(0)
    scale = RET_DK ** -0.5
    QT = 256
    tcol = lax.broadcasted_iota(jnp.int32, (T, 1), 0).astype(F32)

    for hh in range(hp):
        hd = pl.program_id(1) * hp + hh
        cq = slice(hh * RET_DK, (hh + 1) * RET_DK)
        cv = slice(hh * RET_DV, (hh + 1) * RET_DV)
        lgf = _log_sigmoid(jnp.full((1, 1), decay_ref[0, hd], F32))
        lgb = _log_sigmoid(jnp.full((1, 1), decay_ref[1, hd], F32))
        mi = hd if cache_masks else 0

        def _build_mask(lgf=lgf, lgb=lgb, mi=mi):
            for qi in range(T // QT):
                ti = lax.broadcasted_iota(jnp.int32, (QT, T), 0) + qi * QT
                tj = lax.broadcasted_iota(jnp.int32, (QT, T), 1)
                dd = (ti - tj).astype(F32)
                lg = jnp.where(dd > 0, lgf, lgb)
                m_s[mi, qi * QT:(qi + 1) * QT, :] = jnp.where(dd == 0, 2.0 * scale,
                                                               jnp.exp(lg * jnp.abs(dd)) * scale)

        if cache_masks:
            pl.when(b == 0)(_build_mask)
        else:
            _build_mask()

        q = q_ref[:, cq]
        k = k_ref[:, cq]
        v = v_ref[:, cv]
        if latent:
            s0f = s0_ref[0, hh].astype(BF16)
            s0b = s0_ref[1, hh].astype(BF16)
        gnw = gnw_ref[:, cv]
        for qi in range(T // QT):
            rows = slice(qi * QT, (qi + 1) * QT)
            qt = q[rows]
            s = _dot_nt(qt, k) * m_s[mi, rows, :]
            o = _dot(s.astype(BF16), v)
            if latent:
                tq = tcol[rows]
                qf = qt.astype(F32)
                xif = jnp.exp(lgf * (tq + 1.0))
                xib = jnp.exp(lgb * (float(T) - tq))
                o = o + _dot((qf * xif).astype(BF16), s0f) + _dot((qf * xib).astype(BF16), s0b)
            mu = jnp.mean(o, axis=-1, keepdims=True)
            oc = o - mu
            var = jnp.mean(oc * oc, axis=-1, keepdims=True)
            on = oc * lax.rsqrt(var + EPS) * gnw
            oall_s[pl.program_id(1), rows, cv] = (on * z_ref[rows, cv].astype(F32)).astype(BF16)

        if not latent:
            kf = k.astype(F32)
            zf = jnp.exp(lgf * (float(T - 1) - tcol)) * scale
            zb = jnp.exp(lgb * tcol) * scale
            st_ref[0, hh] = _dot_tn((kf * zf).astype(BF16), v)
            st_ref[1, hh] = _dot_tn((kf * zb).astype(BF16), v)

    @pl.when(pl.program_id(1) == pl.num_programs(1) - 1)
    def _finish_sequence():
        _gated_residual(x_ref, mod_ref, 1 + b if latent else 0, wout_ref, oall_s, fnw_ref, y_ref)


def _ret_mix(x, proj, decay, gnw, mods, w_out, T, *, layer, j, latent, hp, fnw=None, s0=None, state_in=None):
    n = proj.shape[0]
    nseq = n // T
    final = fnw is not None
    cache_masks = not latent
    H = RET_HEADS
    ng = H // hp
    wk, wv = hp * RET_DK, hp * RET_DV
    smem = pl.BlockSpec(memory_space=pltpu.SMEM)
    in_specs = [
        smem,
        pl.BlockSpec((T, wk), lambda b, h: (b, h)),
        pl.BlockSpec((T, wk), lambda b, h: (b, ng + h)),
        pl.BlockSpec((T, wv), lambda b, h: (b, ng + h)),
        pl.BlockSpec((T, wv), lambda b, h: (b, 2 * ng + h)),
        pl.BlockSpec((1, wv), lambda b, h: (0, h)),
    ] + _residual_specs(T, RET_V, layer, j, final)
    args = [decay, proj, proj, proj, proj, gnw, x, mods, w_out] + ([fnw] if final else [])
    y_spec = pl.BlockSpec((T, D_MODEL), lambda b, h: (b, 0))
    y_shape = jax.ShapeDtypeStruct((n, D_MODEL), F32)
    st_block = (None, None, 2, hp, RET_DK, RET_DV)
    aliases = {}
    if latent:
        in_specs.append(pl.BlockSpec(st_block, lambda b, h: (b, j, 0, h, 0, 0)))
        args.append(s0)
        out_specs = y_spec
        out_shape = y_shape
    else:
        st_spec = pl.BlockSpec(st_block, lambda b, h: (b, j, 0, h, 0, 0))
        st_shape = jax.ShapeDtypeStruct((nseq, 2, 2, H, RET_DK, RET_DV), F32)
        out_specs = [y_spec, st_spec]
        out_shape = [y_shape, st_shape]
        if state_in is not None:
            in_specs.append(pl.BlockSpec(memory_space=pl.ANY))
            args.append(state_in)
            aliases = {len(args) - 1: 1}
    body = functools.partial(_ret_body, T=T, latent=latent, final=final, cache_masks=cache_masks, hp=hp)
    if state_in is not None:
        inner = body

        def body(*refs):
            n_in = len(args)
            return inner(*refs[:n_in - 1], *refs[n_in:])

    return pl.pallas_call(
        body,
        grid=(nseq, ng),
        in_specs=in_specs,
        out_specs=out_specs,
        out_shape=out_shape,
        scratch_shapes=[
            pltpu.VMEM((H if cache_masks else 1, T, T), F32),
            pltpu.VMEM((ng, T, wv), BF16),
        ],
        input_output_aliases=aliases,
        compiler_params=pltpu.CompilerParams(vmem_limit_bytes=VMEM_LIMIT),
        name="ret_mix_lat" if latent else "ret_mix_ctx",
    )(*args)


def _seg_cumsum(x, pos, axis, reverse):
    n = x.shape[axis]
    s = 1
    while s < SUPER:
        if reverse:
            x = x + jnp.where(pos < SUPER - s, pltpu.roll(x, n - s, axis), 0.0)
        else:
            x = x + jnp.where(pos >= s, pltpu.roll(x, s, axis), 0.0)
        s *= 2
    return x


def _sibling_rows(x, b, sib):
    return jnp.concatenate([x[(2 * m + sib) * b:(2 * m + sib + 1) * b] for m in range(x.shape[0] // (2 * b))], axis=0)


def _tri_inverses(Abs, dirs, eye_s, lm_s, lmh_s, hooks=()):
    n = len(Abs)
    hooks = list(hooks)
    per_level = -(-len(hooks) // (N_LEVELS - 1))
    Xs = [eye_s[...] - Abs[c] * lm_s[dirs[c], 0] for c in range(n)]
    for p in range(1, N_LEVELS):
        b = 1 << p
        if b < BF16_ROWS:
            Ps = [_dot(Xs[c], Abs[c]) for c in range(n)]
            Pm = [Ps[c].astype(BF16) * lm_s[dirs[c], p] for c in range(n)]
            Us = [_dot(Pm[c], Xs[c]) for c in range(n)]
            Xs = [Xs[c] - Us[c].astype(BF16) for c in range(n)]
        else:
            Ps = [_dot(_sibling_rows(Xs[c], b, 1 - dirs[c]), Abs[c]) for c in range(n)]
            Pm = [Ps[c].astype(BF16) * lmh_s[dirs[c], p] for c in range(n)]
            Us = [_dot(Pm[c], Xs[c]).astype(BF16) for c in range(n)]
            for c in range(n):
                d = dirs[c]
                sib = 1 - d
                pieces = []
                for m in range(SUPER // (2 * b)):
                    keep = Xs[c][(2 * m + 1 - sib) * b:(2 * m + 2 - sib) * b]
                    new = Xs[c][(2 * m + sib) * b:(2 * m + sib + 1) * b] - Us[c][m * b:(m + 1) * b]
                    pieces += [new, keep] if d else [keep, new]
                Xs[c] = jnp.concatenate(pieces, axis=0)
        for hook in hooks[(p - 1) * per_level:p * per_level]:
            hook()
    return Xs


def _delta_body(alog_ref, dtb_ref, q_ref, k_ref, v_ref, z_ref, ab_ref, nw_ref, x_ref, mod_ref, wout_ref, *rest,
                T, latent, preconv, final, hp, waves):
    fnw_ref = None
    if final:
        fnw_ref, rest = rest[0], rest[1:]
    if not preconv:
        cwq_ref, cwk_ref, cwv_ref = rest[:3]
        rest = rest[3:]
    if latent:
        s0_ref, y_ref, eye_s, lm_s, lmh_s, tri_s, gt_s, oacc_s, oall_s = rest
    else:
        y_ref, st_ref, eye_s, lm_s, lmh_s, tri_s, gt_s, oacc_s, oall_s = rest
    nsup = T // SUPER
    sup = [slice(r * SUPER, (r + 1) * SUPER) for r in range(nsup)]

    @pl.when((pl.program_id(0) == 0) & (pl.program_id(1) == 0))
    def _init_masks():
        ii = lax.broadcasted_iota(jnp.int32, (SUPER, SUPER), 0)
        jj = lax.broadcasted_iota(jnp.int32, (SUPER, SUPER), 1)
        eye_s[...] = jnp.where(ii == jj, 1.0, 0.0).astype(BF16)
        xor = ii ^ jj
        lvl = jnp.zeros((SUPER, SUPER), jnp.int32)
        for p in range(1, N_LEVELS):
            lvl = lvl + (xor >= (1 << p)).astype(jnp.int32)
        for d in range(2):
            strict = (ii < jj) if d else (ii > jj)
            for p in range(N_LEVELS):
                m = jnp.where(strict & (lvl == p), 1.0, 0.0).astype(BF16)
                lm_s[d, p] = m
                if (1 << p) >= BF16_ROWS:
                    lmh_s[d, p] = _sibling_rows(m, 1 << p, 1 - d)
            tri_s[d] = jnp.where((ii <= jj) if d else (ii >= jj), 1.0, 0.0).astype(BF16)

    trow = lax.broadcasted_iota(jnp.int32, (T, 1), 0)
    pos_c = trow & (SUPER - 1)
    lane = lax.broadcasted_iota(jnp.int32, (T, LANES), 1)

    ab = ab_ref[...]
    g_all = -jnp.exp(alog_ref[...]) * _softplus(ab + dtb_ref[...])
    beta_all = _sigmoid(ab)
    pre = _seg_cumsum(g_all, pos_c, 0, False)
    suf = _seg_cumsum(g_all, pos_c, 0, True)
    G_all = jnp.where(lane >= DEL_HEADS, suf, pre)
    tot_all = pre + suf - g_all
    gt_s[...] = G_all.T

    def column(x, col):
        return jnp.sum(jnp.where(lane == col, x, 0.0), axis=1, keepdims=True)

    def setup_head(hh, dest):
        head = pl.program_id(1) * hp + hh
        cq = slice(hh * DEL_DK, (hh + 1) * DEL_DK)
        cv = slice(hh * DEL_DV, (hh + 1) * DEL_DV)
        if preconv:
            qb = q_ref[:, cq]
            kb = k_ref[:, cq]
            q = qb.astype(F32)
            k = kb.astype(F32)
            v = v_ref[:, cv].astype(F32)
        else:
            q = _silu(_short_conv(q_ref[:, cq].astype(F32), cwq_ref[:, cq], T))
            k = _silu(_short_conv(k_ref[:, cq].astype(F32), cwk_ref[:, cq], T))
            v = _silu(_short_conv(v_ref[:, cv].astype(F32), cwv_ref[:, cv], T))
            q = _l2norm_heads(q, DEL_DK, DEL_DK ** -0.5)
            k = _l2norm_heads(k, DEL_DK, 1.0)
            qb = q.astype(BF16)
            kb = k.astype(BF16)
        kks = [_dot_nt(kb[rows], kb[rows]) for rows in sup]
        qks = [_dot_nt(qb[rows], kb[rows]) for rows in sup]
        for d in range(2):
            col = d * DEL_HEADS + head
            beta = column(beta_all, 2 * DEL_HEADS + col)
            G = column(G_all, col)
            tot = column(tot_all, col)
            G_row = gt_s[pl.ds(col, 1), :]
            for r, rows in enumerate(sup):
                bt = beta[rows]
                Gc = G[rows]
                E = jnp.exp(jnp.minimum(Gc - G_row[:, rows], 0.0))
                pr = dict(hh=hh, d=d, r=r, rows=rows, cv=cv,
                          Ab=(bt * kks[r] * E).astype(BF16),
                          qkl=(qks[r] * E).astype(BF16) * tri_s[d],
                          vb=(v[rows] * bt).astype(BF16),
                          ke=(k[rows] * jnp.exp(tot[rows] - Gc)).astype(BF16))
                if latent:
                    eG = jnp.exp(Gc)
                    pr.update(kw=(k[rows] * (bt * eG)).astype(BF16), qe=q[rows] * eG,
                              et=jnp.exp(tot[r * SUPER:r * SUPER + 1]))
                dest.append(pr)

    def apply_stages(probs, Xs):
        ubs = []

        def stage_u():
            ubs.extend(_dot(X, pr["vb"]).astype(BF16) for X, pr in zip(Xs, probs))

        def stage_o():
            o0s = [_dot(pr["qkl"], ub) for pr, ub in zip(probs, ubs)]
            for pr, o0 in zip(probs, o0s):
                if pr["d"]:
                    oacc_s[pr["rows"], pr["cv"]] = oacc_s[pr["rows"], pr["cv"]] + o0
                else:
                    oacc_s[pr["rows"], pr["cv"]] = o0

        def stage_r():
            Rs = [_dot_tn(pr["ke"], ub) for pr, ub in zip(probs, ubs)]
            for pr, R in zip(probs, Rs):
                if latent:
                    pr["R"] = R
                else:
                    st_ref[pr["d"], pr["hh"]] = R

        def stage_w():
            wbs = [_dot(X, pr["kw"]).astype(BF16) for X, pr in zip(Xs, probs)]
            qts = [(pr["qe"] - _dot(pr["qkl"], wb)).astype(BF16) for pr, wb in zip(probs, wbs)]
            kms = [_dot_tn(pr["ke"], wb).astype(BF16) for pr, wb in zip(probs, wbs)]
            for pr, qt, km in zip(probs, qts, kms):
                pr.update(qt=qt, km=km)

        return [stage_u, stage_o, stage_r] + ([stage_w] if latent else [])

    per_wave = hp // waves
    wave_probs = [[] for _ in range(waves)]
    for hh in range(per_wave):
        setup_head(hh, wave_probs[0])
    pending = []
    for wv in range(waves):
        hooks = list(pending)
        if wv + 1 < waves:
            hooks += [functools.partial(setup_head, hh, wave_probs[wv + 1])
                      for hh in range((wv + 1) * per_wave, (wv + 2) * per_wave)]
        probs = wave_probs[wv]
        Xs = _tri_inverses([pr["Ab"] for pr in probs], [pr["d"] for pr in probs], eye_s, lm_s, lmh_s, hooks)
        pending = apply_stages(probs, Xs)
    for thunk in pending:
        thunk()

    if latent:
        by_key = {(pr["hh"], pr["d"], pr["r"]): pr for probs in wave_probs for pr in probs}
        states = {(hh, d): s0_ref[d, hh] for hh in range(hp) for d in range(2)}
        for step in range(nsup):
            for hh in range(hp):
                for d in range(2):
                    pr = by_key[(hh, d, nsup - 1 - step if d else step)]
                    S = states[(hh, d)]
                    Sb = S.astype(BF16)
                    oacc_s[pr["rows"], pr["cv"]] = oacc_s[pr["rows"], pr["cv"]] + _dot(pr["qt"], Sb)
                    states[(hh, d)] = S * pr["et"] + pr["R"] - _dot(pr["km"], Sb)

    for hh in range(hp):
        cv = slice(hh * DEL_DV, (hh + 1) * DEL_DV)
        o = oacc_s[:, cv]
        o = o * lax.rsqrt(jnp.mean(o * o, axis=-1, keepdims=True) + EPS)
        oall_s[pl.program_id(1), :, cv] = (o * nw_ref[:, cv] * z_ref[:, cv].astype(F32)).astype(BF16)

    @pl.when(pl.program_id(1) == pl.num_programs(1) - 1)
    def _finish_sequence():
        _gated_residual(x_ref, mod_ref, 1 + pl.program_id(0) if latent else 0, wout_ref, oall_s, fnw_ref, y_ref)


def _delta_mix(x, proj, ab, alog, dtb, convw, nw, mods, w_out, T, *, layer, j, hp, waves, latent, preconv, fnw=None,
               s0=None, state_in=None):
    n = proj.shape[0]
    nseq = n // T
    final = fnw is not None
    H = DEL_HEADS
    ng = H // hp
    wk, wv = hp * DEL_DK, hp * DEL_DV
    alog = jnp.pad(alog.reshape(1, 2 * H), ((0, 0), (0, LANES - 2 * H)))
    dtb = jnp.pad(dtb.reshape(1, 2 * H), ((0, 0), (0, LANES - 2 * H)))
    row = pl.BlockSpec((1, LANES), lambda b, h: (0, 0))
    in_specs = [
        row, row,
        pl.BlockSpec((T, wk), lambda b, h: (b, h)),
        pl.BlockSpec((T, wk), lambda b, h: (b, ng + h)),
        pl.BlockSpec((T, wv), lambda b, h: (b, ng + h)),
        pl.BlockSpec((T, wv), lambda b, h: (b, 2 * ng + h)),
        pl.BlockSpec((T, LANES), lambda b, h: (b, 0)),
        pl.BlockSpec((1, wv), lambda b, h: (0, h)),
    ] + _residual_specs(T, DEL_V, layer, j, final)
    args = [alog, dtb, proj, proj, proj, proj, ab, nw, x, mods, w_out] + ([fnw] if final else [])
    if not preconv:
        in_specs += [
            pl.BlockSpec((None, CONV_W, wk), lambda b, h: (j, 0, h)),
            pl.BlockSpec((None, CONV_W, wk), lambda b, h: (j, 0, ng + h)),
            pl.BlockSpec((None, CONV_W, wv), lambda b, h: (j, 0, ng + h)),
        ]
        args += [convw, convw, convw]
    y_spec = pl.BlockSpec((T, D_MODEL), lambda b, h: (b, 0))
    y_shape = jax.ShapeDtypeStruct((n, D_MODEL), F32)
    st_block = (None, None, 2, hp, DEL_DK, DEL_DV)
    aliases = {}
    scratch = [
        pltpu.VMEM((SUPER, SUPER), BF16),
        pltpu.VMEM((2, N_LEVELS, SUPER, SUPER), BF16),
        pltpu.VMEM((2, N_LEVELS, SUPER // 2, SUPER), BF16),
        pltpu.VMEM((2, SUPER, SUPER), BF16),
        pltpu.VMEM((LANES, T), F32),
        pltpu.VMEM((T, wv), F32),
        pltpu.VMEM((ng, T, wv), BF16),
    ]
    if latent:
        in_specs.append(pl.BlockSpec(st_block, lambda b, h: (b, j, 0, h, 0, 0)))
        args.append(s0)
        out_specs = y_spec
        out_shape = y_shape
    else:
        st_spec = pl.BlockSpec(st_block, lambda b, h: (b, j, 0, h, 0, 0))
        st_shape = jax.ShapeDtypeStruct((nseq, 2, 2, H, DEL_DK, DEL_DV), F32)
        out_specs = [y_spec, st_spec]
        out_shape = [y_shape, st_shape]
        if state_in is not None:
            in_specs.append(pl.BlockSpec(memory_space=pl.ANY))
            args.append(state_in)
            aliases = {len(args) - 1: 1}
    body = functools.partial(_delta_body, T=T, latent=latent, preconv=preconv, final=final, hp=hp, waves=waves)
    if state_in is not None:
        inner = body

        def body(*refs):
            n_in = len(args)
            return inner(*refs[:n_in - 1], *refs[n_in:])

    return pl.pallas_call(
        body,
        grid=(nseq, ng),
        in_specs=in_specs,
        out_specs=out_specs,
        out_shape=out_shape,
        scratch_shapes=scratch,
        input_output_aliases=aliases,
        compiler_params=pltpu.CompilerParams(vmem_limit_bytes=VMEM_LIMIT),
        name="delta_mix_lat" if latent else "delta_mix_ctx",
    )(*args)


def _rope_tables(T):
    rows = T // GRID_W
    r = jnp.broadcast_to(jnp.arange(rows)[:, None], (rows, GRID_W)).reshape(T).astype(F32)
    col = jnp.broadcast_to(jnp.arange(GRID_W)[None, :], (rows, GRID_W)).reshape(T).astype(F32)
    n_pairs = RET_DK // 4
    freqs = ROPE_BASE ** (-jnp.arange(n_pairs, dtype=F32) / n_pairs)
    ang = jnp.concatenate([r[:, None] * freqs, col[:, None] * freqs], -1)
    return jnp.cos(ang), jnp.sin(ang)


def _transpose_cast_body(wt_ref, o_ref):
    o_ref[...] = wt_ref[...].T.astype(BF16)


def _cast_del_w_in(w):
    nl = w.shape[0]
    cb = 512
    wt = jnp.swapaxes(w, 1, 2)
    return pl.pallas_call(
        _transpose_cast_body,
        grid=(nl, DEL_MAIN // cb),
        in_specs=[pl.BlockSpec((None, cb, D_MODEL), lambda l, c: (l, c, 0))],
        out_specs=pl.BlockSpec((None, D_MODEL, cb), lambda l, c: (l, 0, c)),
        out_shape=jax.ShapeDtypeStruct((nl, D_MODEL, DEL_MAIN), BF16),
        compiler_params=pltpu.CompilerParams(vmem_limit_bytes=VMEM_LIMIT),
        name="transpose_cast",
    )(wt)


def _prepare_params(norm_w, ret_w_in, ret_w_out, del_w_in, del_w_out, final_norm_w):
    return dict(
        norm_w3=norm_w[:, None, :],
        ret_w_in=ret_w_in.astype(BF16),
        ret_w_out=ret_w_out.astype(BF16),
        del_w_in=_cast_del_w_in(del_w_in),
        del_w_ab=jnp.pad(del_w_in[:, :, DEL_MAIN:], ((0, 0), (0, 0), (0, LANES - DEL_AB))).astype(BF16),
        del_w_out=del_w_out.astype(BF16),
        fnw=final_norm_w[None, :],
    )


def _group_trunk(x, cidx, T, latent, mods, prm, ret_decay, ret_gn_w, del_conv_w, del_a_log, del_dt_bias, del_norm_w,
                 state_ret=None, state_delta=None, depth=DEPTH):
    rope = _rope_tables(T) if latent else None
    preconv = (not latent) and T == ROW_BLOCK
    st_ret = None
    st_del = None
    for i in range(depth):
        j = i // 2
        fnw = prm["fnw"] if i == depth - 1 else None
        if i % 2 == 0:
            proj = _inproj(x, cidx, mods, prm["norm_w3"], prm["ret_w_in"], i, j, "ret_rope" if latent else "ret",
                           rope=rope, blocks_per_seq=T // ROW_BLOCK)
            gnw = ret_gn_w[j][None, :]
            common = dict(layer=i, j=j, latent=latent, fnw=fnw)
            if latent:
                x = _ret_mix(x, proj, ret_decay[j], gnw, mods, prm["ret_w_out"], T, hp=1, s0=state_ret, **common)
            else:
                x, st_ret = _ret_mix(x, proj, ret_decay[j], gnw, mods, prm["ret_w_out"], T, hp=RET_HEADS,
                                     state_in=st_ret, **common)
        else:
            proj, ab = _inproj(x, cidx, mods, prm["norm_w3"], prm["del_w_in"], i, j,
                               "del_conv" if preconv else "del_raw", wab=prm["del_w_ab"], convw=del_conv_w)
            dnw = del_norm_w[j][None, :]
            common = dict(layer=i, j=j, latent=latent, preconv=preconv, fnw=fnw)
            if latent:
                x = _delta_mix(x, proj, ab, del_a_log[j], del_dt_bias[j], del_conv_w, dnw, mods, prm["del_w_out"], T,
                               hp=1, waves=1, s0=state_delta, **common)
            else:
                x, st_del = _delta_mix(x, proj, ab, del_a_log[j], del_dt_bias[j], del_conv_w, dnw, mods,
                                       prm["del_w_out"], T, hp=8, waves=2, state_in=st_del, **common)
    return x, st_ret, st_del


def kernel(x_prompt, x_sample, state_ret, state_delta, c, c_ctx, norm_w, mod_w, mod_b, ret_w_in, ret_decay,
           ret_gn_w, ret_w_out, del_w_in, del_conv_w, del_a_log, del_dt_bias, del_norm_w, del_w_out,
           final_norm_w):
    B, T_ctx, _ = x_prompt.shape
    Bd, T_lat, _ = x_sample.shape
    n_ctx = B * T_ctx
    n_lat = Bd * T_lat

    cond8 = jnp.zeros((8, D_MODEL), F32).at[0].set(c_ctx).at[1:1 + Bd].set(c)
    mods = _modulation(cond8, mod_w, mod_b)
    cidx_ctx = jnp.zeros((n_ctx // ROW_BLOCK,), jnp.int32)
    cidx_lat = 1 + jnp.arange(n_lat // ROW_BLOCK, dtype=jnp.int32) // (T_lat // ROW_BLOCK)
    prm = _prepare_params(norm_w, ret_w_in, ret_w_out, del_w_in, del_w_out, final_norm_w)
    shared = (mods, prm, ret_decay, ret_gn_w, del_conv_w, del_a_log, del_dt_bias, del_norm_w)

    y_ctx, st_ret, st_del = _group_trunk(x_prompt.reshape(n_ctx, D_MODEL), cidx_ctx, T_ctx, False, *shared)
    y_lat, _, _ = _group_trunk(x_sample.reshape(n_lat, D_MODEL), cidx_lat, T_lat, True, *shared,
                               state_ret=state_ret, state_delta=state_delta)
    return (y_ctx.reshape(B, T_ctx, D_MODEL), y_lat.reshape(Bd, T_lat, D_MODEL), st_ret, st_del)
```

```python
import functools

import jax
import jax.numpy as jnp
from jax import lax
from jax.experimental import pallas as pl
from jax.experimental.pallas import tpu as pltpu

F32 = jnp.float32
BF16 = jnp.bfloat16

D_MODEL = 1024
DEPTH = 4
GRID_W = 64
EPS = 1e-6
ROPE_BASE = 10000.0
CONV_W = 3

RET_HEADS = 4
RET_DK = 256
RET_DV = 512
RET_QK = RET_HEADS * RET_DK
RET_V = RET_HEADS * RET_DV
RET_IN = 2 * RET_QK + 2 * RET_V

DEL_HEADS = 8
DEL_DK = 128
DEL_DV = 256
DEL_QK = DEL_HEADS * DEL_DK
DEL_V = DEL_HEADS * DEL_DV
DEL_CONV = 2 * DEL_QK + DEL_V
DEL_MAIN = DEL_CONV + DEL_V
DEL_AB = 4 * DEL_HEADS
assert RET_IN == DEL_MAIN

ROW_BLOCK = 256
COL_GROUP = 512
SUPER = 256
N_LEVELS = SUPER.bit_length() - 1
LANES = 128
SUBLANES = 8
NEG_LOG2_E = -1.4426950408889634
BF16_ROWS = 16
VMEM_LIMIT = 56 * 1024 * 1024


def _sigmoid(x):
    return 1.0 / (1.0 + jnp.exp2(x * NEG_LOG2_E))


def _silu(x):
    return x * _sigmoid(x)


def _softplus(x):
    return jnp.maximum(x, 0.0) + jnp.log1p(jnp.exp(-jnp.abs(x)))


def _log_sigmoid(x):
    return jnp.minimum(x, 0.0) - jnp.log1p(jnp.exp(-jnp.abs(x)))


def _dot(a, b):
    return jnp.dot(a, b, preferred_element_type=F32)


def _dot_nt(a, b):
    return lax.dot_general(a, b, (((1,), (1,)), ((), ())), preferred_element_type=F32)


def _dot_tn(a, b):
    return lax.dot_general(a, b, (((0,), (0,)), ((), ())), preferred_element_type=F32)


def _short_conv(x, w, T):
    y = w[0:1, :] * pltpu.roll(x, 1, 0) + w[1:2, :] * x + w[2:3, :] * pltpu.roll(x, T - 1, 0)
    r8 = lax.broadcasted_iota(jnp.int32, (SUBLANES, 1), 0)
    head = y[:SUBLANES] - jnp.where(r8 == 0, w[0:1, :] * x[T - 1:T, :], 0.0)
    tail = y[T - SUBLANES:] - jnp.where(r8 == SUBLANES - 1, w[2:3, :] * x[0:1, :], 0.0)
    return jnp.concatenate([head, y[SUBLANES:T - SUBLANES], tail], axis=0)


def _l2norm_heads(x, width, scale):
    out = []
    for s in range(0, x.shape[1], width):
        xs = x[:, s:s + width]
        out.append(xs * (lax.rsqrt(jnp.sum(xs * xs, axis=-1, keepdims=True) + EPS) * scale))
    return jnp.concatenate(out, axis=-1)


def _mod_body(cond_ref, w_ref, b_ref, o_ref):
    sc = _silu(cond_ref[...])
    o_ref[...] = jnp.dot(sc, w_ref[...], preferred_element_type=F32,
                         precision=lax.Precision.HIGHEST) + b_ref[...]


def _modulation(cond8, mod_w, mod_b):
    nb = 3
    return pl.pallas_call(
        _mod_body,
        grid=(DEPTH, nb),
        in_specs=[
            pl.BlockSpec((8, D_MODEL), lambda i, j: (0, 0)),
            pl.BlockSpec((None, D_MODEL, D_MODEL), lambda i, j: (i, 0, j)),
            pl.BlockSpec((None, 1, D_MODEL), lambda i, j: (i, 0, j)),
        ],
        out_specs=pl.BlockSpec((None, 8, D_MODEL), lambda i, j: (i, 0, j)),
        out_shape=jax.ShapeDtypeStruct((DEPTH, 8, 3 * D_MODEL), F32),
        compiler_params=pltpu.CompilerParams(vmem_limit_bytes=VMEM_LIMIT),
        name="modulation",
    )(cond8, mod_w, mod_b.reshape(DEPTH, 1, 3 * D_MODEL))


def _inproj_body(cidx_ref, x_ref, mod_ref, nw_ref, w_ref, *rest, mode):
    if mode == "ret":
        (o_ref,) = rest
    elif mode == "ret_rope":
        cos_ref, sin_ref, o_ref = rest
    elif mode == "del_conv":
        wab_ref, cw_ref, o_ref, ab_ref = rest
    else:
        wab_ref, o_ref, ab_ref = rest
    ci = cidx_ref[pl.program_id(0)]
    x = x_ref[...]
    ms = jnp.mean(x * x, axis=-1, keepdims=True)
    xn = x * lax.rsqrt(ms + EPS) * nw_ref[...]
    m = mod_ref[pl.ds(ci, 1), :]
    shift = m[:, :D_MODEL]
    scale = m[:, D_MODEL:2 * D_MODEL]
    h = (xn * (1.0 + scale) + shift).astype(BF16)
    if mode.startswith("del"):
        ab_ref[...] = _dot(h, wab_ref[...])
    gate_start = 2 * RET_QK + RET_V
    for c0 in range(0, RET_IN, COL_GROUP):
        cols = slice(c0, c0 + COL_GROUP)
        p = _dot(h, w_ref[:, cols])
        if c0 >= gate_start:
            p = _silu(p)
        elif mode == "ret_rope" and c0 < 2 * RET_QK:
            cos = cos_ref[...]
            sin = sin_ref[...]
            hk = RET_DK // 2
            out = []
            for s in range(0, COL_GROUP, RET_DK):
                x1, x2 = p[:, s:s + hk], p[:, s + hk:s + RET_DK]
                out += [x1 * cos - x2 * sin, x1 * sin + x2 * cos]
            p = jnp.concatenate(out, axis=-1)
        elif mode == "del_conv":
            p = _silu(_short_conv(p, cw_ref[:, cols], ROW_BLOCK))
            if c0 < DEL_QK:
                p = _l2norm_heads(p, DEL_DK, DEL_DK ** -0.5)
            elif c0 < 2 * DEL_QK:
                p = _l2norm_heads(p, DEL_DK, 1.0)
        o_ref[:, cols] = p.astype(BF16)


def _inproj(x, cidx, mods, norm_w3, w, layer, j, mode, *, wab=None, convw=None, rope=None, blocks_per_seq=1):
    n = x.shape[0]
    ncols = w.shape[2]
    in_specs = [
        pl.BlockSpec((ROW_BLOCK, D_MODEL), lambda i, c: (i, 0)),
        pl.BlockSpec((None, 8, 3 * D_MODEL), lambda i, c: (layer, 0, 0)),
        pl.BlockSpec((None, 1, D_MODEL), lambda i, c: (layer, 0, 0)),
        pl.BlockSpec((None, D_MODEL, ncols), lambda i, c: (j, 0, 0)),
    ]
    out_specs = [pl.BlockSpec((ROW_BLOCK, RET_IN), lambda i, c: (i, 0))]
    out_shape = [jax.ShapeDtypeStruct((n, RET_IN), BF16)]
    args = [x, mods, norm_w3, w]
    if mode == "ret_rope":
        cos, sin = rope
        spec = pl.BlockSpec((ROW_BLOCK, LANES), lambda i, c: (i % blocks_per_seq, 0))
        in_specs += [spec, spec]
        args += [cos, sin]
    if mode.startswith("del"):
        in_specs.append(pl.BlockSpec((None, D_MODEL, LANES), lambda i, c: (j, 0, 0)))
        args.append(wab)
        if mode == "del_conv":
            in_specs.append(pl.BlockSpec((None, CONV_W, DEL_CONV), lambda i, c: (j, 0, 0)))
            args.append(convw)
        out_specs.append(pl.BlockSpec((ROW_BLOCK, LANES), lambda i, c: (i, 0)))
        out_shape.append(jax.ShapeDtypeStruct((n, LANES), F32))
    res = pl.pallas_call(
        functools.partial(_inproj_body, mode=mode),
        grid_spec=pltpu.PrefetchScalarGridSpec(
            num_scalar_prefetch=1, grid=(n // ROW_BLOCK,), in_specs=in_specs, out_specs=out_specs),
        out_shape=out_shape,
        compiler_params=pltpu.CompilerParams(vmem_limit_bytes=VMEM_LIMIT),
        name="inproj_" + mode,
    )(cidx, *args)
    return res if mode.startswith("del") else res[0]


def _gated_residual(x_ref, mod_ref, cond_row, wout_ref, oall_s, fnw_ref, y_ref):
    ng, _, width = oall_s.shape
    acc = _dot(oall_s[0], wout_ref[0:width, :])
    for g in range(1, ng):
        acc = acc + _dot(oall_s[g], wout_ref[g * width:(g + 1) * width, :])
    gate = mod_ref[pl.ds(cond_row, 1), :][:, 2 * D_MODEL:]
    xn = x_ref[...] + gate * acc
    if fnw_ref is not None:
        ms = jnp.mean(xn * xn, axis=-1, keepdims=True)
        xn = xn * lax.rsqrt(ms + EPS) * fnw_ref[...]
    y_ref[...] = xn


def _residual_specs(T, vdim, layer, j, final):
    specs = [
        pl.BlockSpec((T, D_MODEL), lambda b, h: (b, 0)),
        pl.BlockSpec((None, 8, 3 * D_MODEL), lambda b, h: (layer, 0, 0)),
        pl.BlockSpec((None, vdim, D_MODEL), lambda b, h: (j, 0, 0)),
    ]
    if final:
        specs.append(pl.BlockSpec((1, D_MODEL), lambda b, h: (0, 0)))
    return specs


def _ret_body(decay_ref, q_ref, k_ref, v_ref, z_ref, gnw_ref, x_ref, mod_ref, wout_ref, *rest, T, latent, final,
              cache_masks, hp):
    fnw_ref = None
    if final:
        fnw_ref, rest = rest[0], rest[1:]
    if latent:
        s0_ref, y_ref, m_s, oall_s = rest
    else:
        y_ref, st_ref, m_s, oall_s = rest
    b = pl.program_id(0)
    scale = RET_DK ** -0.5
    QT = 256
    tcol = lax.broadcasted_iota(jnp.int32, (T, 1), 0).astype(F32)

    heads = []
    for hh in range(hp):
        hd = pl.program_id(1) * hp + hh
        lgf = _log_sigmoid(jnp.full((1, 1), decay_ref[0, hd], F32))
        lgb = _log_sigmoid(jnp.full((1, 1), decay_ref[1, hd], F32))
        mi = hd if cache_masks else 0

        def _build_mask(lgf=lgf, lgb=lgb, mi=mi):
            for qi in range(T // QT):
                ti = lax.broadcasted_iota(jnp.int32, (QT, T), 0) + qi * QT
                tj = lax.broadcasted_iota(jnp.int32, (QT, T), 1)
                dd = (ti - tj).astype(F32)
                lg = jnp.where(dd > 0, lgf, lgb)
                m_s[mi, qi * QT:(qi + 1) * QT, :] = jnp.where(dd == 0, 2.0 * scale,
                                                               jnp.exp(lg * jnp.abs(dd)) * scale)

        if cache_masks:
            pl.when(b == 0)(_build_mask)
        else:
            _build_mask()
        heads.append(dict(hh=hh, lgf=lgf, lgb=lgb, mi=mi,
                          cq=slice(hh * RET_DK, (hh + 1) * RET_DK), cv=slice(hh * RET_DV, (hh + 1) * RET_DV)))

    for qi in range(T // QT):
        rows = slice(qi * QT, (qi + 1) * QT)
        ss = [_dot_nt(q_ref[rows, hd["cq"]], k_ref[:, hd["cq"]]) * m_s[hd["mi"], rows, :] for hd in heads]
        os = [_dot(s.astype(BF16), v_ref[:, hd["cv"]]) for s, hd in zip(ss, heads)]
        if latent:
            tq = tcol[rows]
            for c, hd in enumerate(heads):
                qf = q_ref[rows, hd["cq"]].astype(F32)
                xif = jnp.exp(hd["lgf"] * (tq + 1.0))
                xib = jnp.exp(hd["lgb"] * (float(T) - tq))
                os[c] = (os[c] + _dot((qf * xif).astype(BF16), s0_ref[0, hd["hh"]].astype(BF16))
                         + _dot((qf * xib).astype(BF16), s0_ref[1, hd["hh"]].astype(BF16)))
        for o, hd in zip(os, heads):
            mu = jnp.mean(o, axis=-1, keepdims=True)
            oc = o - mu
            var = jnp.mean(oc * oc, axis=-1, keepdims=True)
            on = oc * lax.rsqrt(var + EPS) * gnw_ref[:, hd["cv"]]
            oall_s[pl.program_id(1), rows, hd["cv"]] = (on * z_ref[rows, hd["cv"]].astype(F32)).astype(BF16)

    if not latent:
        kzs = []
        for hd in heads:
            kf = k_ref[:, hd["cq"]].astype(F32)
            zf = jnp.exp(hd["lgf"] * (float(T - 1) - tcol)) * scale
            zb = jnp.exp(hd["lgb"] * tcol) * scale
            kzs.append(((kf * zf).astype(BF16), (kf * zb).astype(BF16)))
        sts = [(_dot_tn(kz[0], v_ref[:, hd["cv"]]), _dot_tn(kz[1], v_ref[:, hd["cv"]])) for kz, hd in zip(kzs, heads)]
        for st, hd in zip(sts, heads):
            st_ref[0, hd["hh"]] = st[0]
            st_ref[1, hd["hh"]] = st[1]

    @pl.when(pl.program_id(1) == pl.num_programs(1) - 1)
    def _finish_sequence():
        _gated_residual(x_ref, mod_ref, 1 + b if latent else 0, wout_ref, oall_s, fnw_ref, y_ref)


def _ret_mix(x, proj, decay, gnw, mods, w_out, T, *, layer, j, latent, hp, fnw=None, s0=None, state_in=None):
    n = proj.shape[0]
    nseq = n // T
    final = fnw is not None
    cache_masks = not latent
    H = RET_HEADS
    ng = H // hp
    wk, wv = hp * RET_DK, hp * RET_DV
    smem = pl.BlockSpec(memory_space=pltpu.SMEM)
    in_specs = [
        smem,
        pl.BlockSpec((T, wk), lambda b, h: (b, h)),
        pl.BlockSpec((T, wk), lambda b, h: (b, ng + h)),
        pl.BlockSpec((T, wv), lambda b, h: (b, ng + h)),
        pl.BlockSpec((T, wv), lambda b, h: (b, 2 * ng + h)),
        pl.BlockSpec((1, wv), lambda b, h: (0, h)),
    ] + _residual_specs(T, RET_V, layer, j, final)
    args = [decay, proj, proj, proj, proj, gnw, x, mods, w_out] + ([fnw] if final else [])
    y_spec = pl.BlockSpec((T, D_MODEL), lambda b, h: (b, 0))
    y_shape = jax.ShapeDtypeStruct((n, D_MODEL), F32)
    st_block = (None, None, 2, hp, RET_DK, RET_DV)
    aliases = {}
    if latent:
        in_specs.append(pl.BlockSpec(st_block, lambda b, h: (b, j, 0, h, 0, 0)))
        args.append(s0)
        out_specs = y_spec
        out_shape = y_shape
    else:
        st_spec = pl.BlockSpec(st_block, lambda b, h: (b, j, 0, h, 0, 0))
        st_shape = jax.ShapeDtypeStruct((nseq, 2, 2, H, RET_DK, RET_DV), F32)
        out_specs = [y_spec, st_spec]
        out_shape = [y_shape, st_shape]
        if state_in is not None:
            in_specs.append(pl.BlockSpec(memory_space=pl.ANY))
            args.append(state_in)
            aliases = {len(args) - 1: 1}
    body = functools.partial(_ret_body, T=T, latent=latent, final=final, cache_masks=cache_masks, hp=hp)
    if state_in is not None:
        inner = body

        def body(*refs):
            n_in = len(args)
            return inner(*refs[:n_in - 1], *refs[n_in:])

    return pl.pallas_call(
        body,
        grid=(nseq, ng),
        in_specs=in_specs,
        out_specs=out_specs,
        out_shape=out_shape,
        scratch_shapes=[
            pltpu.VMEM((H if cache_masks else 1, T, T), F32),
            pltpu.VMEM((ng, T, wv), BF16),
        ],
        input_output_aliases=aliases,
        compiler_params=pltpu.CompilerParams(vmem_limit_bytes=VMEM_LIMIT),
        name="ret_mix_lat" if latent else "ret_mix_ctx",
    )(*args)


def _seg_cumsum(x, pos, axis, reverse):
    n = x.shape[axis]
    s = 1
    while s < SUPER:
        if reverse:
            x = x + jnp.where(pos < SUPER - s, pltpu.roll(x, n - s, axis), 0.0)
        else:
            x = x + jnp.where(pos >= s, pltpu.roll(x, s, axis), 0.0)
        s *= 2
    return x


def _sibling_rows(x, b, sib):
    return jnp.concatenate([x[(2 * m + sib) * b:(2 * m + sib + 1) * b] for m in range(x.shape[0] // (2 * b))], axis=0)


def _tri_inverses(Abs, dirs, eye_s, lm_s, lmh_s, hooks=()):
    n = len(Abs)
    hooks = list(hooks)
    per_level = -(-len(hooks) // (N_LEVELS - 1))
    Xs = [eye_s[...] - Abs[c] * lm_s[dirs[c], 0] for c in range(n)]
    for p in range(1, N_LEVELS):
        b = 1 << p
        if b < BF16_ROWS:
            Ps = [_dot(Xs[c], Abs[c]) for c in range(n)]
            Pm = [Ps[c].astype(BF16) * lm_s[dirs[c], p] for c in range(n)]
            Us = [_dot(Pm[c], Xs[c]) for c in range(n)]
            Xs = [Xs[c] - Us[c].astype(BF16) for c in range(n)]
        else:
            Ps = [_dot(_sibling_rows(Xs[c], b, 1 - dirs[c]), Abs[c]) for c in range(n)]
            Pm = [Ps[c].astype(BF16) * lmh_s[dirs[c], p] for c in range(n)]
            Us = [_dot(Pm[c], Xs[c]).astype(BF16) for c in range(n)]
            for c in range(n):
                d = dirs[c]
                sib = 1 - d
                pieces = []
                for m in range(SUPER // (2 * b)):
                    keep = Xs[c][(2 * m + 1 - sib) * b:(2 * m + 2 - sib) * b]
                    new = Xs[c][(2 * m + sib) * b:(2 * m + sib + 1) * b] - Us[c][m * b:(m + 1) * b]
                    pieces += [new, keep] if d else [keep, new]
                Xs[c] = jnp.concatenate(pieces, axis=0)
        for hook in hooks[(p - 1) * per_level:p * per_level]:
            hook()
    return Xs


def _delta_body(alog_ref, dtb_ref, q_ref, k_ref, v_ref, z_ref, ab_ref, nw_ref, x_ref, mod_ref, wout_ref, *rest,
                T, latent, preconv, final, hp, waves):
    fnw_ref = None
    if final:
        fnw_ref, rest = rest[0], rest[1:]
    if not preconv:
        cwq_ref, cwk_ref, cwv_ref = rest[:3]
        rest = rest[3:]
    if latent:
        s0_ref, y_ref, eye_s, lm_s, lmh_s, tri_s, gt_s, oacc_s, oall_s = rest
    else:
        y_ref, st_ref, eye_s, lm_s, lmh_s, tri_s, gt_s, oacc_s, oall_s = rest
    nsup = T // SUPER
    sup = [slice(r * SUPER, (r + 1) * SUPER) for r in range(nsup)]

    @pl.when((pl.program_id(0) == 0) & (pl.program_id(1) == 0))
    def _init_masks():
        ii = lax.broadcasted_iota(jnp.int32, (SUPER, SUPER), 0)
        jj = lax.broadcasted_iota(jnp.int32, (SUPER, SUPER), 1)
        eye_s[...] = jnp.where(ii == jj, 1.0, 0.0).astype(BF16)
        xor = ii ^ jj
        lvl = jnp.zeros((SUPER, SUPER), jnp.int32)
        for p in range(1, N_LEVELS):
            lvl = lvl + (xor >= (1 << p)).astype(jnp.int32)
        for d in range(2):
            strict = (ii < jj) if d else (ii > jj)
            for p in range(N_LEVELS):
                m = jnp.where(strict & (lvl == p), 1.0, 0.0).astype(BF16)
                lm_s[d, p] = m
                if (1 << p) >= BF16_ROWS:
                    lmh_s[d, p] = _sibling_rows(m, 1 << p, 1 - d)
            tri_s[d] = jnp.where((ii <= jj) if d else (ii >= jj), 1.0, 0.0).astype(BF16)

    trow = lax.broadcasted_iota(jnp.int32, (T, 1), 0)
    pos_c = trow & (SUPER - 1)
    lane = lax.broadcasted_iota(jnp.int32, (T, LANES), 1)

    ab = ab_ref[...]
    g_all = (NEG_LOG2_E * jnp.exp(alog_ref[...])) * _softplus(ab + dtb_ref[...])
    beta_all = _sigmoid(ab)
    pre = _seg_cumsum(g_all, pos_c, 0, False)
    suf = _seg_cumsum(g_all, pos_c, 0, True)
    G_all = jnp.where(lane >= DEL_HEADS, suf, pre)
    tot_all = pre + suf - g_all
    gt_s[...] = G_all.T

    def column(x, col):
        return jnp.sum(jnp.where(lane == col, x, 0.0), axis=1, keepdims=True)

    def setup_head(hh, dest):
        head = pl.program_id(1) * hp + hh
        cq = slice(hh * DEL_DK, (hh + 1) * DEL_DK)
        cv = slice(hh * DEL_DV, (hh + 1) * DEL_DV)
        if preconv:
            qb = q_ref[:, cq]
            kb = k_ref[:, cq]
            q = qb.astype(F32)
            k = kb.astype(F32)
            v = v_ref[:, cv].astype(F32)
        else:
            q = _silu(_short_conv(q_ref[:, cq].astype(F32), cwq_ref[:, cq], T))
            k = _silu(_short_conv(k_ref[:, cq].astype(F32), cwk_ref[:, cq], T))
            v = _silu(_short_conv(v_ref[:, cv].astype(F32), cwv_ref[:, cv], T))
            q = _l2norm_heads(q, DEL_DK, DEL_DK ** -0.5)
            k = _l2norm_heads(k, DEL_DK, 1.0)
            qb = q.astype(BF16)
            kb = k.astype(BF16)
        kks = [_dot_nt(kb[rows], kb[rows]) for rows in sup]
        qks = [_dot_nt(qb[rows], kb[rows]) for rows in sup]
        for d in range(2):
            col = d * DEL_HEADS + head
            beta = column(beta_all, 2 * DEL_HEADS + col)
            G = column(G_all, col)
            tot = column(tot_all, col)
            G_row = gt_s[pl.ds(col, 1), :]
            for r, rows in enumerate(sup):
                bt = beta[rows]
                Gc = G[rows]
                E = jnp.exp2(jnp.minimum(Gc - G_row[:, rows], 0.0))
                pr = dict(hh=hh, d=d, r=r, rows=rows, cv=cv,
                          Ab=(bt * kks[r] * E).astype(BF16),
                          qkl=(qks[r] * E).astype(BF16) * tri_s[d],
                          vb=(v[rows] * bt).astype(BF16),
                          ke=(k[rows] * jnp.exp2(tot[rows] - Gc)).astype(BF16))
                if latent:
                    eG = jnp.exp2(Gc)
                    pr.update(kw=(k[rows] * (bt * eG)).astype(BF16), qe=q[rows] * eG,
                              et=jnp.exp2(tot[r * SUPER:r * SUPER + 1]))
                dest.append(pr)

    def apply_stages(probs, Xs):
        ubs = []

        def stage_u():
            ubs.extend(_dot(X, pr["vb"]).astype(BF16) for X, pr in zip(Xs, probs))

        def stage_o():
            o0s = [_dot(pr["qkl"], ub) for pr, ub in zip(probs, ubs)]
            for pr, o0 in zip(probs, o0s):
                if pr["d"]:
                    oacc_s[pr["rows"], pr["cv"]] = oacc_s[pr["rows"], pr["cv"]] + o0
                else:
                    oacc_s[pr["rows"], pr["cv"]] = o0

        def stage_r():
            Rs = [_dot_tn(pr["ke"], ub) for pr, ub in zip(probs, ubs)]
            for pr, R in zip(probs, Rs):
                if latent:
                    pr["R"] = R
                else:
                    st_ref[pr["d"], pr["hh"]] = R

        def stage_w():
            wbs = [_dot(X, pr["kw"]).astype(BF16) for X, pr in zip(Xs, probs)]
            qts = [(pr["qe"] - _dot(pr["qkl"], wb)).astype(BF16) for pr, wb in zip(probs, wbs)]
            kms = [_dot_tn(pr["ke"], wb).astype(BF16) for pr, wb in zip(probs, wbs)]
            for pr, qt, km in zip(probs, qts, kms):
                pr.update(qt=qt, km=km)

        return [stage_u, stage_o, stage_r] + ([stage_w] if latent else [])

    per_wave = hp // waves
    wave_probs = [[] for _ in range(waves)]
    for hh in range(per_wave):
        setup_head(hh, wave_probs[0])
    pending = []
    for wv in range(waves):
        hooks = list(pending)
        if wv + 1 < waves:
            hooks += [functools.partial(setup_head, hh, wave_probs[wv + 1])
                      for hh in range((wv + 1) * per_wave, (wv + 2) * per_wave)]
        probs = wave_probs[wv]
        Xs = _tri_inverses([pr["Ab"] for pr in probs], [pr["d"] for pr in probs], eye_s, lm_s, lmh_s, hooks)
        pending = apply_stages(probs, Xs)
    for thunk in pending:
        thunk()

    if latent:
        by_key = {(pr["hh"], pr["d"], pr["r"]): pr for probs in wave_probs for pr in probs}
        states = {(hh, d): s0_ref[d, hh] for hh in range(hp) for d in range(2)}
        for step in range(nsup):
            for hh in range(hp):
                for d in range(2):
                    pr = by_key[(hh, d, nsup - 1 - step if d else step)]
                    S = states[(hh, d)]
                    Sb = S.astype(BF16)
                    oacc_s[pr["rows"], pr["cv"]] = oacc_s[pr["rows"], pr["cv"]] + _dot(pr["qt"], Sb)
                    states[(hh, d)] = S * pr["et"] + pr["R"] - _dot(pr["km"], Sb)

    for hh in range(hp):
        cv = slice(hh * DEL_DV, (hh + 1) * DEL_DV)
        o = oacc_s[:, cv]
        o = o * lax.rsqrt(jnp.mean(o * o, axis=-1, keepdims=True) + EPS)
        oall_s[pl.program_id(1), :, cv] = (o * nw_ref[:, cv] * z_ref[:, cv].astype(F32)).astype(BF16)

    @pl.when(pl.program_id(1) == pl.num_programs(1) - 1)
    def _finish_sequence():
        _gated_residual(x_ref, mod_ref, 1 + pl.program_id(0) if latent else 0, wout_ref, oall_s, fnw_ref, y_ref)


def _delta_mix(x, proj, ab, alog, dtb, convw, nw, mods, w_out, T, *, layer, j, hp, waves, latent, preconv, fnw=None,
               s0=None, state_in=None):
    n = proj.shape[0]
    nseq = n // T
    final = fnw is not None
    H = DEL_HEADS
    ng = H // hp
    wk, wv = hp * DEL_DK, hp * DEL_DV
    alog = jnp.pad(alog.reshape(1, 2 * H), ((0, 0), (0, LANES - 2 * H)))
    dtb = jnp.pad(dtb.reshape(1, 2 * H), ((0, 0), (0, LANES - 2 * H)))
    row = pl.BlockSpec((1, LANES), lambda b, h: (0, 0))
    in_specs = [
        row, row,
        pl.BlockSpec((T, wk), lambda b, h: (b, h)),
        pl.BlockSpec((T, wk), lambda b, h: (b, ng + h)),
        pl.BlockSpec((T, wv), lambda b, h: (b, ng + h)),
        pl.BlockSpec((T, wv), lambda b, h: (b, 2 * ng + h)),
        pl.BlockSpec((T, LANES), lambda b, h: (b, 0)),
        pl.BlockSpec((1, wv), lambda b, h: (0, h)),
    ] + _residual_specs(T, DEL_V, layer, j, final)
    args = [alog, dtb, proj, proj, proj, proj, ab, nw, x, mods, w_out] + ([fnw] if final else [])
    if not preconv:
        in_specs += [
            pl.BlockSpec((None, CONV_W, wk), lambda b, h: (j, 0, h)),
            pl.BlockSpec((None, CONV_W, wk), lambda b, h: (j, 0, ng + h)),
            pl.BlockSpec((None, CONV_W, wv), lambda b, h: (j, 0, ng + h)),
        ]
        args += [convw, convw, convw]
    y_spec = pl.BlockSpec((T, D_MODEL), lambda b, h: (b, 0))
    y_shape = jax.ShapeDtypeStruct((n, D_MODEL), F32)
    st_block = (None, None, 2, hp, DEL_DK, DEL_DV)
    aliases = {}
    scratch = [
        pltpu.VMEM((SUPER, SUPER), BF16),
        pltpu.VMEM((2, N_LEVELS, SUPER, SUPER), BF16),
        pltpu.VMEM((2, N_LEVELS, SUPER // 2, SUPER), BF16),
        pltpu.VMEM((2, SUPER, SUPER), BF16),
        pltpu.VMEM((LANES, T), F32),
        pltpu.VMEM((T, wv), F32),
        pltpu.VMEM((ng, T, wv), BF16),
    ]
    if latent:
        in_specs.append(pl.BlockSpec(st_block, lambda b, h: (b, j, 0, h, 0, 0)))
        args.append(s0)
        out_specs = y_spec
        out_shape = y_shape
    else:
        st_spec = pl.BlockSpec(st_block, lambda b, h: (b, j, 0, h, 0, 0))
        st_shape = jax.ShapeDtypeStruct((nseq, 2, 2, H, DEL_DK, DEL_DV), F32)
        out_specs = [y_spec, st_spec]
        out_shape = [y_shape, st_shape]
        if state_in is not None:
            in_specs.append(pl.BlockSpec(memory_space=pl.ANY))
            args.append(state_in)
            aliases = {len(args) - 1: 1}
    body = functools.partial(_delta_body, T=T, latent=latent, preconv=preconv, final=final, hp=hp, waves=waves)
    if state_in is not None:
        inner = body

        def body(*refs):
            n_in = len(args)
            return inner(*refs[:n_in - 1], *refs[n_in:])

    return pl.pallas_call(
        body,
        grid=(nseq, ng),
        in_specs=in_specs,
        out_specs=out_specs,
        out_shape=out_shape,
        scratch_shapes=scratch,
        input_output_aliases=aliases,
        compiler_params=pltpu.CompilerParams(vmem_limit_bytes=VMEM_LIMIT),
        name="delta_mix_lat" if latent else "delta_mix_ctx",
    )(*args)


def _rope_tables(T):
    rows = T // GRID_W
    r = jnp.broadcast_to(jnp.arange(rows)[:, None], (rows, GRID_W)).reshape(T).astype(F32)
    col = jnp.broadcast_to(jnp.arange(GRID_W)[None, :], (rows, GRID_W)).reshape(T).astype(F32)
    n_pairs = RET_DK // 4
    freqs = ROPE_BASE ** (-jnp.arange(n_pairs, dtype=F32) / n_pairs)
    ang = jnp.concatenate([r[:, None] * freqs, col[:, None] * freqs], -1)
    return jnp.cos(ang), jnp.sin(ang)


def _transpose_cast_body(wt_ref, o_ref):
    o_ref[...] = wt_ref[...].T.astype(BF16)


def _cast_del_w_in(w):
    nl = w.shape[0]
    cb = 512
    wt = jnp.swapaxes(w, 1, 2)
    return pl.pallas_call(
        _transpose_cast_body,
        grid=(nl, DEL_MAIN // cb),
        in_specs=[pl.BlockSpec((None, cb, D_MODEL), lambda l, c: (l, c, 0))],
        out_specs=pl.BlockSpec((None, D_MODEL, cb), lambda l, c: (l, 0, c)),
        out_shape=jax.ShapeDtypeStruct((nl, D_MODEL, DEL_MAIN), BF16),
        compiler_params=pltpu.CompilerParams(vmem_limit_bytes=VMEM_LIMIT),
        name="transpose_cast",
    )(wt)


def _prepare_params(norm_w, ret_w_in, ret_w_out, del_w_in, del_w_out, final_norm_w):
    return dict(
        norm_w3=norm_w[:, None, :],
        ret_w_in=ret_w_in.astype(BF16),
        ret_w_out=ret_w_out.astype(BF16),
        del_w_in=_cast_del_w_in(del_w_in),
        del_w_ab=jnp.pad(del_w_in[:, :, DEL_MAIN:], ((0, 0), (0, 0), (0, LANES - DEL_AB))).astype(BF16),
        del_w_out=del_w_out.astype(BF16),
        fnw=final_norm_w[None, :],
    )


def _group_trunk(x, cidx, T, latent, mods, prm, ret_decay, ret_gn_w, del_conv_w, del_a_log, del_dt_bias, del_norm_w,
                 state_ret=None, state_delta=None, depth=DEPTH):
    rope = _rope_tables(T) if latent else None
    preconv = (not latent) and T == ROW_BLOCK
    st_ret = None
    st_del = None
    for i in range(depth):
        j = i // 2
        fnw = prm["fnw"] if i == depth - 1 else None
        if i % 2 == 0:
            proj = _inproj(x, cidx, mods, prm["norm_w3"], prm["ret_w_in"], i, j, "ret_rope" if latent else "ret",
                           rope=rope, blocks_per_seq=T // ROW_BLOCK)
            gnw = ret_gn_w[j][None, :]
            common = dict(layer=i, j=j, latent=latent, fnw=fnw)
            if latent:
                x = _ret_mix(x, proj, ret_decay[j], gnw, mods, prm["ret_w_out"], T, hp=1, s0=state_ret, **common)
            else:
                x, st_ret = _ret_mix(x, proj, ret_decay[j], gnw, mods, prm["ret_w_out"], T, hp=RET_HEADS,
                                     state_in=st_ret, **common)
        else:
            proj, ab = _inproj(x, cidx, mods, prm["norm_w3"], prm["del_w_in"], i, j,
                               "del_conv" if preconv else "del_raw", wab=prm["del_w_ab"], convw=del_conv_w)
            dnw = del_norm_w[j][None, :]
            common = dict(layer=i, j=j, latent=latent, preconv=preconv, fnw=fnw)
            if latent:
                x = _delta_mix(x, proj, ab, del_a_log[j], del_dt_bias[j], del_conv_w, dnw, mods, prm["del_w_out"], T,
                               hp=1, waves=1, s0=state_delta, **common)
            else:
                x, st_del = _delta_mix(x, proj, ab, del_a_log[j], del_dt_bias[j], del_conv_w, dnw, mods,
                                       prm["del_w_out"], T, hp=8, waves=2, state_in=st_del, **common)
    return x, st_ret, st_del


def kernel(x_prompt, x_sample, state_ret, state_delta, c, c_ctx, norm_w, mod_w, mod_b, ret_w_in, ret_decay,
           ret_gn_w, ret_w_out, del_w_in, del_conv_w, del_a_log, del_dt_bias, del_norm_w, del_w_out,
           final_norm_w):
    B, T_ctx, _ = x_prompt.shape
    Bd, T_lat, _ = x_sample.shape
    n_ctx = B * T_ctx
    n_lat = Bd * T_lat

    cond8 = jnp.zeros((8, D_MODEL), F32).at[0].set(c_ctx).at[1:1 + Bd].set(c)
    mods = _modulation(cond8, mod_w, mod_b)
    cidx_ctx = jnp.zeros((n_ctx // ROW_BLOCK,), jnp.int32)
    cidx_lat = 1 + jnp.arange(n_lat // ROW_BLOCK, dtype=jnp.int32) // (T_lat // ROW_BLOCK)
    prm = _prepare_params(norm_w, ret_w_in, ret_w_out, del_w_in, del_w_out, final_norm_w)
    shared = (mods, prm, ret_decay, ret_gn_w, del_conv_w, del_a_log, del_dt_bias, del_norm_w)

    y_ctx, st_ret, st_del = _group_trunk(x_prompt.reshape(n_ctx, D_MODEL), cidx_ctx, T_ctx, False, *shared)
    y_lat, _, _ = _group_trunk(x_sample.reshape(n_lat, D_MODEL), cidx_lat, T_lat, True, *shared,
                               state_ret=state_ret, state_delta=state_delta)
    return (y_ctx.reshape(B, T_ctx, D_MODEL), y_lat.reshape(Bd, T_lat, D_MODEL), st_ret, st_del)
```

```python
import functools

import jax
import jax.numpy as jnp
from jax import lax
from jax.experimental import pallas as pl
from jax.experimental.pallas import tpu as pltpu

F32 = jnp.float32
BF16 = jnp.bfloat16

D_MODEL = 1024
DEPTH = 4
GRID_W = 64
EPS = 1e-6
ROPE_BASE = 10000.0
CONV_W = 3

RET_HEADS = 4
RET_DK = 256
RET_DV = 512
RET_QK = RET_HEADS * RET_DK
RET_V = RET_HEADS * RET_DV
RET_IN = 2 * RET_QK + 2 * RET_V

DEL_HEADS = 8
DEL_DK = 128
DEL_DV = 256
DEL_QK = DEL_HEADS * DEL_DK
DEL_V = DEL_HEADS * DEL_DV
DEL_CONV = 2 * DEL_QK + DEL_V
DEL_MAIN = DEL_CONV + DEL_V
DEL_AB = 4 * DEL_HEADS
assert RET_IN == DEL_MAIN

ROW_BLOCK = 256
COL_GROUP = 512
SUPER = 256
N_LEVELS = SUPER.bit_length() - 1
LANES = 128
SUBLANES = 8
NEG_LOG2_E = -1.4426950408889634
BF16_ROWS = 16
VMEM_LIMIT = 56 * 1024 * 1024


def _sigmoid(x):
    return 1.0 / (1.0 + jnp.exp2(x * NEG_LOG2_E))


def _silu(x):
    return x * _sigmoid(x)


def _softplus(x):
    return jnp.maximum(x, 0.0) + jnp.log1p(jnp.exp(-jnp.abs(x)))


def _log_sigmoid(x):
    return jnp.minimum(x, 0.0) - jnp.log1p(jnp.exp(-jnp.abs(x)))


def _dot(a, b):
    return jnp.dot(a, b, preferred_element_type=F32)


def _dot_nt(a, b):
    return lax.dot_general(a, b, (((1,), (1,)), ((), ())), preferred_element_type=F32)


def _dot_tn(a, b):
    return lax.dot_general(a, b, (((0,), (0,)), ((), ())), preferred_element_type=F32)


def _short_conv(x, w, T):
    y = w[0:1, :] * pltpu.roll(x, 1, 0) + w[1:2, :] * x + w[2:3, :] * pltpu.roll(x, T - 1, 0)
    r8 = lax.broadcasted_iota(jnp.int32, (SUBLANES, 1), 0)
    head = y[:SUBLANES] - jnp.where(r8 == 0, w[0:1, :] * x[T - 1:T, :], 0.0)
    tail = y[T - SUBLANES:] - jnp.where(r8 == SUBLANES - 1, w[2:3, :] * x[0:1, :], 0.0)
    return jnp.concatenate([head, y[SUBLANES:T - SUBLANES], tail], axis=0)


def _l2norm_heads(x, width, scale):
    out = []
    for s in range(0, x.shape[1], width):
        xs = x[:, s:s + width]
        out.append(xs * (lax.rsqrt(jnp.sum(xs * xs, axis=-1, keepdims=True) + EPS) * scale))
    return jnp.concatenate(out, axis=-1)


def _mod_body(cond_ref, w_ref, b_ref, o_ref):
    sc = _silu(cond_ref[...])
    o_ref[...] = jnp.dot(sc, w_ref[...], preferred_element_type=F32,
                         precision=lax.Precision.HIGHEST) + b_ref[...]


def _modulation(cond8, mod_w, mod_b):
    nb = 3
    return pl.pallas_call(
        _mod_body,
        grid=(DEPTH, nb),
        in_specs=[
            pl.BlockSpec((8, D_MODEL), lambda i, j: (0, 0)),
            pl.BlockSpec((None, D_MODEL, D_MODEL), lambda i, j: (i, 0, j)),
            pl.BlockSpec((None, 1, D_MODEL), lambda i, j: (i, 0, j)),
        ],
        out_specs=pl.BlockSpec((None, 8, D_MODEL), lambda i, j: (i, 0, j)),
        out_shape=jax.ShapeDtypeStruct((DEPTH, 8, 3 * D_MODEL), F32),
        compiler_params=pltpu.CompilerParams(vmem_limit_bytes=VMEM_LIMIT),
        name="modulation",
    )(cond8, mod_w, mod_b.reshape(DEPTH, 1, 3 * D_MODEL))


def _inproj_body(cidx_ref, x_ref, mod_ref, nw_ref, w_ref, *rest, mode):
    if mode == "ret":
        (o_ref,) = rest
    elif mode == "ret_rope":
        cos_ref, sin_ref, o_ref = rest
    elif mode == "del_conv":
        wab_ref, cw_ref, o_ref, ab_ref = rest
    else:
        wab_ref, o_ref, ab_ref = rest
    ci = cidx_ref[pl.program_id(0)]
    x = x_ref[...]
    ms = jnp.mean(x * x, axis=-1, keepdims=True)
    xn = x * lax.rsqrt(ms + EPS) * nw_ref[...]
    m = mod_ref[pl.ds(ci, 1), :]
    shift = m[:, :D_MODEL]
    scale = m[:, D_MODEL:2 * D_MODEL]
    h = (xn * (1.0 + scale) + shift).astype(BF16)
    if mode.startswith("del"):
        ab_ref[...] = _dot(h, wab_ref[...])
    gate_start = 2 * RET_QK + RET_V
    for c0 in range(0, RET_IN, COL_GROUP):
        cols = slice(c0, c0 + COL_GROUP)
        p = _dot(h, w_ref[:, cols])
        if c0 >= gate_start:
            p = _silu(p)
        elif mode == "ret_rope" and c0 < 2 * RET_QK:
            cos = cos_ref[...]
            sin = sin_ref[...]
            hk = RET_DK // 2
            out = []
            for s in range(0, COL_GROUP, RET_DK):
                x1, x2 = p[:, s:s + hk], p[:, s + hk:s + RET_DK]
                out += [x1 * cos - x2 * sin, x1 * sin + x2 * cos]
            p = jnp.concatenate(out, axis=-1)
        elif mode == "del_conv" and c0 < 2 * DEL_QK:
            p = _silu(_short_conv(p, cw_ref[:, cols], ROW_BLOCK))
            p = _l2norm_heads(p, DEL_DK, DEL_DK ** -0.5 if c0 < DEL_QK else 1.0)
        o_ref[:, cols] = p.astype(BF16)


def _inproj(x, cidx, mods, norm_w3, w, layer, j, mode, *, wab=None, convw=None, rope=None, blocks_per_seq=1):
    n = x.shape[0]
    ncols = w.shape[2]
    in_specs = [
        pl.BlockSpec((ROW_BLOCK, D_MODEL), lambda i, c: (i, 0)),
        pl.BlockSpec((None, 8, 3 * D_MODEL), lambda i, c: (layer, 0, 0)),
        pl.BlockSpec((None, 1, D_MODEL), lambda i, c: (layer, 0, 0)),
        pl.BlockSpec((None, D_MODEL, ncols), lambda i, c: (j, 0, 0)),
    ]
    out_specs = [pl.BlockSpec((ROW_BLOCK, RET_IN), lambda i, c: (i, 0))]
    out_shape = [jax.ShapeDtypeStruct((n, RET_IN), BF16)]
    args = [x, mods, norm_w3, w]
    if mode == "ret_rope":
        cos, sin = rope
        spec = pl.BlockSpec((ROW_BLOCK, LANES), lambda i, c: (i % blocks_per_seq, 0))
        in_specs += [spec, spec]
        args += [cos, sin]
    if mode.startswith("del"):
        in_specs.append(pl.BlockSpec((None, D_MODEL, LANES), lambda i, c: (j, 0, 0)))
        args.append(wab)
        if mode == "del_conv":
            in_specs.append(pl.BlockSpec((None, CONV_W, DEL_CONV), lambda i, c: (j, 0, 0)))
            args.append(convw)
        out_specs.append(pl.BlockSpec((ROW_BLOCK, LANES), lambda i, c: (i, 0)))
        out_shape.append(jax.ShapeDtypeStruct((n, LANES), F32))
    res = pl.pallas_call(
        functools.partial(_inproj_body, mode=mode),
        grid_spec=pltpu.PrefetchScalarGridSpec(
            num_scalar_prefetch=1, grid=(n // ROW_BLOCK,), in_specs=in_specs, out_specs=out_specs),
        out_shape=out_shape,
        compiler_params=pltpu.CompilerParams(vmem_limit_bytes=VMEM_LIMIT),
        name="inproj_" + mode,
    )(cidx, *args)
    return res if mode.startswith("del") else res[0]


def _gated_residual(x_ref, mod_ref, cond_row, wout_ref, oall_s, fnw_ref, y_ref):
    ng, _, width = oall_s.shape
    acc = _dot(oall_s[0], wout_ref[0:width, :])
    for g in range(1, ng):
        acc = acc + _dot(oall_s[g], wout_ref[g * width:(g + 1) * width, :])
    gate = mod_ref[pl.ds(cond_row, 1), :][:, 2 * D_MODEL:]
    xn = x_ref[...] + gate * acc
    if fnw_ref is not None:
        ms = jnp.mean(xn * xn, axis=-1, keepdims=True)
        xn = xn * lax.rsqrt(ms + EPS) * fnw_ref[...]
    y_ref[...] = xn


def _residual_specs(T, vdim, layer, j, final):
    specs = [
        pl.BlockSpec((T, D_MODEL), lambda b, h: (b, 0)),
        pl.BlockSpec((None, 8, 3 * D_MODEL), lambda b, h: (layer, 0, 0)),
        pl.BlockSpec((None, vdim, D_MODEL), lambda b, h: (j, 0, 0)),
    ]
    if final:
        specs.append(pl.BlockSpec((1, D_MODEL), lambda b, h: (0, 0)))
    return specs


def _ret_body(decay_ref, q_ref, k_ref, v_ref, z_ref, gnw_ref, x_ref, mod_ref, wout_ref, *rest, T, latent, final,
              cache_masks, hp):
    fnw_ref = None
    if final:
        fnw_ref, rest = rest[0], rest[1:]
    if latent:
        s0_ref, y_ref, m_s, oall_s = rest
    else:
        y_ref, st_ref, m_s, oall_s = rest
    b = pl.program_id(0)
    scale = RET_DK ** -0.5
    QT = 256
    tcol = lax.broadcasted_iota(jnp.int32, (T, 1), 0).astype(F32)

    heads = []
    for hh in range(hp):
        hd = pl.program_id(1) * hp + hh
        lgf = _log_sigmoid(jnp.full((1, 1), decay_ref[0, hd], F32))
        lgb = _log_sigmoid(jnp.full((1, 1), decay_ref[1, hd], F32))
        mi = hd if cache_masks else 0

        def _build_mask(lgf=lgf, lgb=lgb, mi=mi):
            for qi in range(T // QT):
                ti = lax.broadcasted_iota(jnp.int32, (QT, T), 0) + qi * QT
                tj = lax.broadcasted_iota(jnp.int32, (QT, T), 1)
                dd = (ti - tj).astype(F32)
                lg = jnp.where(dd > 0, lgf, lgb)
                m_s[mi, qi * QT:(qi + 1) * QT, :] = jnp.where(dd == 0, 2.0 * scale,
                                                               jnp.exp(lg * jnp.abs(dd)) * scale)

        if cache_masks:
            pl.when(b == 0)(_build_mask)
        else:
            _build_mask()
        heads.append(dict(hh=hh, lgf=lgf, lgb=lgb, mi=mi,
                          cq=slice(hh * RET_DK, (hh + 1) * RET_DK), cv=slice(hh * RET_DV, (hh + 1) * RET_DV)))

    for qi in range(T // QT):
        rows = slice(qi * QT, (qi + 1) * QT)
        ss = [_dot_nt(q_ref[rows, hd["cq"]], k_ref[:, hd["cq"]]) * m_s[hd["mi"], rows, :] for hd in heads]
        os = [_dot(s.astype(BF16), v_ref[:, hd["cv"]]) for s, hd in zip(ss, heads)]
        if latent:
            tq = tcol[rows]
            for c, hd in enumerate(heads):
                qf = q_ref[rows, hd["cq"]].astype(F32)
                xif = jnp.exp(hd["lgf"] * (tq + 1.0))
                xib = jnp.exp(hd["lgb"] * (float(T) - tq))
                os[c] = (os[c] + _dot((qf * xif).astype(BF16), s0_ref[0, hd["hh"]].astype(BF16))
                         + _dot((qf * xib).astype(BF16), s0_ref[1, hd["hh"]].astype(BF16)))
        for o, hd in zip(os, heads):
            mu = jnp.mean(o, axis=-1, keepdims=True)
            oc = o - mu
            var = jnp.mean(oc * oc, axis=-1, keepdims=True)
            on = oc * lax.rsqrt(var + EPS) * gnw_ref[:, hd["cv"]]
            oall_s[pl.program_id(1), rows, hd["cv"]] = (on * z_ref[rows, hd["cv"]].astype(F32)).astype(BF16)

    if not latent:
        kzs = []
        for hd in heads:
            kf = k_ref[:, hd["cq"]].astype(F32)
            zf = jnp.exp(hd["lgf"] * (float(T - 1) - tcol)) * scale
            zb = jnp.exp(hd["lgb"] * tcol) * scale
            kzs.append(((kf * zf).astype(BF16), (kf * zb).astype(BF16)))
        sts = [(_dot_tn(kz[0], v_ref[:, hd["cv"]]), _dot_tn(kz[1], v_ref[:, hd["cv"]])) for kz, hd in zip(kzs, heads)]
        for st, hd in zip(sts, heads):
            st_ref[0, hd["hh"]] = st[0]
            st_ref[1, hd["hh"]] = st[1]

    @pl.when(pl.program_id(1) == pl.num_programs(1) - 1)
    def _finish_sequence():
        _gated_residual(x_ref, mod_ref, 1 + b if latent else 0, wout_ref, oall_s, fnw_ref, y_ref)


def _ret_mix(x, proj, decay, gnw, mods, w_out, T, *, layer, j, latent, hp, fnw=None, s0=None, state_in=None):
    n = proj.shape[0]
    nseq = n // T
    final = fnw is not None
    cache_masks = not latent
    H = RET_HEADS
    ng = H // hp
    wk, wv = hp * RET_DK, hp * RET_DV
    smem = pl.BlockSpec(memory_space=pltpu.SMEM)
    in_specs = [
        smem,
        pl.BlockSpec((T, wk), lambda b, h: (b, h)),
        pl.BlockSpec((T, wk), lambda b, h: (b, ng + h)),
        pl.BlockSpec((T, wv), lambda b, h: (b, ng + h)),
        pl.BlockSpec((T, wv), lambda b, h: (b, 2 * ng + h)),
        pl.BlockSpec((1, wv), lambda b, h: (0, h)),
    ] + _residual_specs(T, RET_V, layer, j, final)
    args = [decay, proj, proj, proj, proj, gnw, x, mods, w_out] + ([fnw] if final else [])
    y_spec = pl.BlockSpec((T, D_MODEL), lambda b, h: (b, 0))
    y_shape = jax.ShapeDtypeStruct((n, D_MODEL), F32)
    st_block = (None, None, 2, hp, RET_DK, RET_DV)
    aliases = {}
    if latent:
        in_specs.append(pl.BlockSpec(st_block, lambda b, h: (b, j, 0, h, 0, 0)))
        args.append(s0)
        out_specs = y_spec
        out_shape = y_shape
    else:
        st_spec = pl.BlockSpec(st_block, lambda b, h: (b, j, 0, h, 0, 0))
        st_shape = jax.ShapeDtypeStruct((nseq, 2, 2, H, RET_DK, RET_DV), F32)
        out_specs = [y_spec, st_spec]
        out_shape = [y_shape, st_shape]
        if state_in is not None:
            in_specs.append(pl.BlockSpec(memory_space=pl.ANY))
            args.append(state_in)
            aliases = {len(args) - 1: 1}
    body = functools.partial(_ret_body, T=T, latent=latent, final=final, cache_masks=cache_masks, hp=hp)
    if state_in is not None:
        inner = body

        def body(*refs):
            n_in = len(args)
            return inner(*refs[:n_in - 1], *refs[n_in:])

    return pl.pallas_call(
        body,
        grid=(nseq, ng),
        in_specs=in_specs,
        out_specs=out_specs,
        out_shape=out_shape,
        scratch_shapes=[
            pltpu.VMEM((H if cache_masks else 1, T, T), F32),
            pltpu.VMEM((ng, T, wv), BF16),
        ],
        input_output_aliases=aliases,
        compiler_params=pltpu.CompilerParams(vmem_limit_bytes=VMEM_LIMIT),
        name="ret_mix_lat" if latent else "ret_mix_ctx",
    )(*args)


def _seg_cumsum(x, pos, axis, reverse):
    n = x.shape[axis]
    s = 1
    while s < SUPER:
        if reverse:
            x = x + jnp.where(pos < SUPER - s, pltpu.roll(x, n - s, axis), 0.0)
        else:
            x = x + jnp.where(pos >= s, pltpu.roll(x, s, axis), 0.0)
        s *= 2
    return x


def _sibling_rows(x, b, sib):
    return jnp.concatenate([x[(2 * m + sib) * b:(2 * m + sib + 1) * b] for m in range(x.shape[0] // (2 * b))], axis=0)


def _tri_inverses(Abs, dirs, eye_s, lm_s, lmh_s, hooks=()):
    n = len(Abs)
    hooks = list(hooks)
    per_level = -(-len(hooks) // (N_LEVELS - 1))
    Xs = [eye_s[...] - Abs[c] * lm_s[dirs[c], 0] for c in range(n)]
    for p in range(1, N_LEVELS):
        b = 1 << p
        if b < BF16_ROWS:
            Ps = [_dot(Xs[c], Abs[c]) for c in range(n)]
            Pm = [Ps[c].astype(BF16) * lm_s[dirs[c], p] for c in range(n)]
            Us = [_dot(Pm[c], Xs[c]) for c in range(n)]
            Xs = [Xs[c] - Us[c].astype(BF16) for c in range(n)]
        else:
            Ps = [_dot(_sibling_rows(Xs[c], b, 1 - dirs[c]), Abs[c]) for c in range(n)]
            Pm = [Ps[c].astype(BF16) * lmh_s[dirs[c], p] for c in range(n)]
            Us = [_dot(Pm[c], Xs[c]).astype(BF16) for c in range(n)]
            for c in range(n):
                d = dirs[c]
                sib = 1 - d
                pieces = []
                for m in range(SUPER // (2 * b)):
                    keep = Xs[c][(2 * m + 1 - sib) * b:(2 * m + 2 - sib) * b]
                    new = Xs[c][(2 * m + sib) * b:(2 * m + sib + 1) * b] - Us[c][m * b:(m + 1) * b]
                    pieces += [new, keep] if d else [keep, new]
                Xs[c] = jnp.concatenate(pieces, axis=0)
        for hook in hooks[(p - 1) * per_level:p * per_level]:
            hook()
    return Xs


def _delta_body(alog_ref, dtb_ref, q_ref, k_ref, v_ref, z_ref, ab_ref, nw_ref, x_ref, mod_ref, wout_ref, *rest,
                T, latent, preconv, final, hp, waves):
    fnw_ref = None
    if final:
        fnw_ref, rest = rest[0], rest[1:]
    cwv_ref, rest = rest[0], rest[1:]
    if not preconv:
        cwq_ref, cwk_ref = rest[:2]
        rest = rest[2:]
    if latent:
        s0_ref, y_ref, eye_s, lm_s, lmh_s, tri_s, gt_s, oacc_s, oall_s = rest
    else:
        y_ref, st_ref, eye_s, lm_s, lmh_s, tri_s, gt_s, oacc_s, oall_s = rest
    nsup = T // SUPER
    sup = [slice(r * SUPER, (r + 1) * SUPER) for r in range(nsup)]

    @pl.when((pl.program_id(0) == 0) & (pl.program_id(1) == 0))
    def _init_masks():
        ii = lax.broadcasted_iota(jnp.int32, (SUPER, SUPER), 0)
        jj = lax.broadcasted_iota(jnp.int32, (SUPER, SUPER), 1)
        eye_s[...] = jnp.where(ii == jj, 1.0, 0.0).astype(BF16)
        xor = ii ^ jj
        lvl = jnp.zeros((SUPER, SUPER), jnp.int32)
        for p in range(1, N_LEVELS):
            lvl = lvl + (xor >= (1 << p)).astype(jnp.int32)
        for d in range(2):
            strict = (ii < jj) if d else (ii > jj)
            for p in range(N_LEVELS):
                m = jnp.where(strict & (lvl == p), 1.0, 0.0).astype(BF16)
                lm_s[d, p] = m
                if (1 << p) >= BF16_ROWS:
                    lmh_s[d, p] = _sibling_rows(m, 1 << p, 1 - d)
            tri_s[d] = jnp.where((ii <= jj) if d else (ii >= jj), 1.0, 0.0).astype(BF16)

    trow = lax.broadcasted_iota(jnp.int32, (T, 1), 0)
    pos_c = trow & (SUPER - 1)
    lane = lax.broadcasted_iota(jnp.int32, (T, LANES), 1)

    ab = ab_ref[...]
    g_all = (NEG_LOG2_E * jnp.exp(alog_ref[...])) * _softplus(ab + dtb_ref[...])
    beta_all = _sigmoid(ab)
    pre = _seg_cumsum(g_all, pos_c, 0, False)
    suf = _seg_cumsum(g_all, pos_c, 0, True)
    G_all = jnp.where(lane >= DEL_HEADS, suf, pre)
    tot_all = pre + suf - g_all
    gt_s[...] = G_all.T

    def column(x, col):
        return jnp.sum(jnp.where(lane == col, x, 0.0), axis=1, keepdims=True)

    def setup_head(hh, dest):
        head = pl.program_id(1) * hp + hh
        cq = slice(hh * DEL_DK, (hh + 1) * DEL_DK)
        cv = slice(hh * DEL_DV, (hh + 1) * DEL_DV)
        v = _silu(_short_conv(v_ref[:, cv].astype(F32), cwv_ref[:, cv], T))
        if preconv:
            qb = q_ref[:, cq]
            kb = k_ref[:, cq]
            q = qb.astype(F32)
            k = kb.astype(F32)
        else:
            q = _silu(_short_conv(q_ref[:, cq].astype(F32), cwq_ref[:, cq], T))
            k = _silu(_short_conv(k_ref[:, cq].astype(F32), cwk_ref[:, cq], T))
            q = _l2norm_heads(q, DEL_DK, DEL_DK ** -0.5)
            k = _l2norm_heads(k, DEL_DK, 1.0)
            qb = q.astype(BF16)
            kb = k.astype(BF16)
        kks = [_dot_nt(kb[rows], kb[rows]) for rows in sup]
        qks = [_dot_nt(qb[rows], kb[rows]) for rows in sup]
        for d in range(2):
            col = d * DEL_HEADS + head
            beta = column(beta_all, 2 * DEL_HEADS + col)
            G = column(G_all, col)
            tot = column(tot_all, col)
            G_row = gt_s[pl.ds(col, 1), :]
            for r, rows in enumerate(sup):
                bt = beta[rows]
                Gc = G[rows]
                E = jnp.exp2(jnp.minimum(Gc - G_row[:, rows], 0.0))
                pr = dict(hh=hh, d=d, r=r, rows=rows, cv=cv,
                          Ab=(bt * kks[r] * E).astype(BF16),
                          qkl=(qks[r] * E).astype(BF16) * tri_s[d],
                          vb=(v[rows] * bt).astype(BF16),
                          ke=(k[rows] * jnp.exp2(tot[rows] - Gc)).astype(BF16))
                if latent:
                    eG = jnp.exp2(Gc)
                    pr.update(kw=(k[rows] * (bt * eG)).astype(BF16), qe=q[rows] * eG,
                              et=jnp.exp2(tot[r * SUPER:r * SUPER + 1]))
                dest.append(pr)

    def apply_stages(probs, Xs):
        ubs = []

        def stage_u():
            ubs.extend(_dot(X, pr["vb"]).astype(BF16) for X, pr in zip(Xs, probs))

        def stage_o():
            o0s = [_dot(pr["qkl"], ub) for pr, ub in zip(probs, ubs)]
            for pr, o0 in zip(probs, o0s):
                if pr["d"]:
                    oacc_s[pr["rows"], pr["cv"]] = oacc_s[pr["rows"], pr["cv"]] + o0
                else:
                    oacc_s[pr["rows"], pr["cv"]] = o0

        def stage_r():
            Rs = [_dot_tn(pr["ke"], ub) for pr, ub in zip(probs, ubs)]
            for pr, R in zip(probs, Rs):
                if latent:
                    pr["R"] = R
                else:
                    st_ref[pr["d"], pr["hh"]] = R

        def stage_w():
            wbs = [_dot(X, pr["kw"]).astype(BF16) for X, pr in zip(Xs, probs)]
            qts = [(pr["qe"] - _dot(pr["qkl"], wb)).astype(BF16) for pr, wb in zip(probs, wbs)]
            kms = [_dot_tn(pr["ke"], wb).astype(BF16) for pr, wb in zip(probs, wbs)]
            for pr, qt, km in zip(probs, qts, kms):
                pr.update(qt=qt, km=km)

        return [stage_u, stage_o, stage_r] + ([stage_w] if latent else [])

    per_wave = hp // waves
    wave_probs = [[] for _ in range(waves)]
    for hh in range(per_wave):
        setup_head(hh, wave_probs[0])
    pending = []
    for wv in range(waves):
        hooks = list(pending)
        if wv + 1 < waves:
            hooks += [functools.partial(setup_head, hh, wave_probs[wv + 1])
                      for hh in range((wv + 1) * per_wave, (wv + 2) * per_wave)]
        probs = wave_probs[wv]
        Xs = _tri_inverses([pr["Ab"] for pr in probs], [pr["d"] for pr in probs], eye_s, lm_s, lmh_s, hooks)
        pending = apply_stages(probs, Xs)
    for thunk in pending:
        thunk()

    if latent:
        by_key = {(pr["hh"], pr["d"], pr["r"]): pr for probs in wave_probs for pr in probs}
        states = {(hh, d): s0_ref[d, hh] for hh in range(hp) for d in range(2)}
        for step in range(nsup):
            for hh in range(hp):
                for d in range(2):
                    pr = by_key[(hh, d, nsup - 1 - step if d else step)]
                    S = states[(hh, d)]
                    Sb = S.astype(BF16)
                    oacc_s[pr["rows"], pr["cv"]] = oacc_s[pr["rows"], pr["cv"]] + _dot(pr["qt"], Sb)
                    states[(hh, d)] = S * pr["et"] + pr["R"] - _dot(pr["km"], Sb)

    for hh in range(hp):
        cv = slice(hh * DEL_DV, (hh + 1) * DEL_DV)
        o = oacc_s[:, cv]
        o = o * lax.rsqrt(jnp.mean(o * o, axis=-1, keepdims=True) + EPS)
        oall_s[pl.program_id(1), :, cv] = (o * nw_ref[:, cv] * z_ref[:, cv].astype(F32)).astype(BF16)

    @pl.when(pl.program_id(1) == pl.num_programs(1) - 1)
    def _finish_sequence():
        _gated_residual(x_ref, mod_ref, 1 + pl.program_id(0) if latent else 0, wout_ref, oall_s, fnw_ref, y_ref)


def _delta_mix(x, proj, ab, alog, dtb, convw, nw, mods, w_out, T, *, layer, j, hp, waves, latent, preconv, fnw=None,
               s0=None, state_in=None):
    n = proj.shape[0]
    nseq = n // T
    final = fnw is not None
    H = DEL_HEADS
    ng = H // hp
    wk, wv = hp * DEL_DK, hp * DEL_DV
    alog = jnp.pad(alog.reshape(1, 2 * H), ((0, 0), (0, LANES - 2 * H)))
    dtb = jnp.pad(dtb.reshape(1, 2 * H), ((0, 0), (0, LANES - 2 * H)))
    row = pl.BlockSpec((1, LANES), lambda b, h: (0, 0))
    in_specs = [
        row, row,
        pl.BlockSpec((T, wk), lambda b, h: (b, h)),
        pl.BlockSpec((T, wk), lambda b, h: (b, ng + h)),
        pl.BlockSpec((T, wv), lambda b, h: (b, ng + h)),
        pl.BlockSpec((T, wv), lambda b, h: (b, 2 * ng + h)),
        pl.BlockSpec((T, LANES), lambda b, h: (b, 0)),
        pl.BlockSpec((1, wv), lambda b, h: (0, h)),
    ] + _residual_specs(T, DEL_V, layer, j, final)
    args = [alog, dtb, proj, proj, proj, proj, ab, nw, x, mods, w_out] + ([fnw] if final else [])
    in_specs.append(pl.BlockSpec((None, CONV_W, wv), lambda b, h: (j, 0, ng + h)))
    args.append(convw)
    if not preconv:
        in_specs += [
            pl.BlockSpec((None, CONV_W, wk), lambda b, h: (j, 0, h)),
            pl.BlockSpec((None, CONV_W, wk), lambda b, h: (j, 0, ng + h)),
        ]
        args += [convw, convw]
    y_spec = pl.BlockSpec((T, D_MODEL), lambda b, h: (b, 0))
    y_shape = jax.ShapeDtypeStruct((n, D_MODEL), F32)
    st_block = (None, None, 2, hp, DEL_DK, DEL_DV)
    aliases = {}
    scratch = [
        pltpu.VMEM((SUPER, SUPER), BF16),
        pltpu.VMEM((2, N_LEVELS, SUPER, SUPER), BF16),
        pltpu.VMEM((2, N_LEVELS, SUPER // 2, SUPER), BF16),
        pltpu.VMEM((2, SUPER, SUPER), BF16),
        pltpu.VMEM((LANES, T), F32),
        pltpu.VMEM((T, wv), F32),
        pltpu.VMEM((ng, T, wv), BF16),
    ]
    if latent:
        in_specs.append(pl.BlockSpec(st_block, lambda b, h: (b, j, 0, h, 0, 0)))
        args.append(s0)
        out_specs = y_spec
        out_shape = y_shape
    else:
        st_spec = pl.BlockSpec(st_block, lambda b, h: (b, j, 0, h, 0, 0))
        st_shape = jax.ShapeDtypeStruct((nseq, 2, 2, H, DEL_DK, DEL_DV), F32)
        out_specs = [y_spec, st_spec]
        out_shape = [y_shape, st_shape]
        if state_in is not None:
            in_specs.append(pl.BlockSpec(memory_space=pl.ANY))
            args.append(state_in)
            aliases = {len(args) - 1: 1}
    body = functools.partial(_delta_body, T=T, latent=latent, preconv=preconv, final=final, hp=hp, waves=waves)
    if state_in is not None:
        inner = body

        def body(*refs):
            n_in = len(args)
            return inner(*refs[:n_in - 1], *refs[n_in:])

    return pl.pallas_call(
        body,
        grid=(nseq, ng),
        in_specs=in_specs,
        out_specs=out_specs,
        out_shape=out_shape,
        scratch_shapes=scratch,
        input_output_aliases=aliases,
        compiler_params=pltpu.CompilerParams(vmem_limit_bytes=VMEM_LIMIT),
        name="delta_mix_lat" if latent else "delta_mix_ctx",
    )(*args)


def _rope_tables(T):
    rows = T // GRID_W
    r = jnp.broadcast_to(jnp.arange(rows)[:, None], (rows, GRID_W)).reshape(T).astype(F32)
    col = jnp.broadcast_to(jnp.arange(GRID_W)[None, :], (rows, GRID_W)).reshape(T).astype(F32)
    n_pairs = RET_DK // 4
    freqs = ROPE_BASE ** (-jnp.arange(n_pairs, dtype=F32) / n_pairs)
    ang = jnp.concatenate([r[:, None] * freqs, col[:, None] * freqs], -1)
    return jnp.cos(ang), jnp.sin(ang)


def _transpose_cast_body(wt_ref, o_ref):
    o_ref[...] = wt_ref[...].T.astype(BF16)


def _cast_del_w_in(w):
    nl = w.shape[0]
    cb = 512
    wt = jnp.swapaxes(w, 1, 2)
    return pl.pallas_call(
        _transpose_cast_body,
        grid=(nl, DEL_MAIN // cb),
        in_specs=[pl.BlockSpec((None, cb, D_MODEL), lambda l, c: (l, c, 0))],
        out_specs=pl.BlockSpec((None, D_MODEL, cb), lambda l, c: (l, 0, c)),
        out_shape=jax.ShapeDtypeStruct((nl, D_MODEL, DEL_MAIN), BF16),
        compiler_params=pltpu.CompilerParams(vmem_limit_bytes=VMEM_LIMIT),
        name="transpose_cast",
    )(wt)


def _prepare_params(norm_w, ret_w_in, ret_w_out, del_w_in, del_w_out, final_norm_w):
    return dict(
        norm_w3=norm_w[:, None, :],
        ret_w_in=ret_w_in.astype(BF16),
        ret_w_out=ret_w_out.astype(BF16),
        del_w_in=_cast_del_w_in(del_w_in),
        del_w_ab=jnp.pad(del_w_in[:, :, DEL_MAIN:], ((0, 0), (0, 0), (0, LANES - DEL_AB))).astype(BF16),
        del_w_out=del_w_out.astype(BF16),
        fnw=final_norm_w[None, :],
    )


def _group_trunk(x, cidx, T, latent, mods, prm, ret_decay, ret_gn_w, del_conv_w, del_a_log, del_dt_bias, del_norm_w,
                 state_ret=None, state_delta=None, depth=DEPTH):
    rope = _rope_tables(T) if latent else None
    preconv = (not latent) and T == ROW_BLOCK
    st_ret = None
    st_del = None
    for i in range(depth):
        j = i // 2
        fnw = prm["fnw"] if i == depth - 1 else None
        if i % 2 == 0:
            proj = _inproj(x, cidx, mods, prm["norm_w3"], prm["ret_w_in"], i, j, "ret_rope" if latent else "ret",
                           rope=rope, blocks_per_seq=T // ROW_BLOCK)
            gnw = ret_gn_w[j][None, :]
            common = dict(layer=i, j=j, latent=latent, fnw=fnw)
            if latent:
                x = _ret_mix(x, proj, ret_decay[j], gnw, mods, prm["ret_w_out"], T, hp=1, s0=state_ret, **common)
            else:
                x, st_ret = _ret_mix(x, proj, ret_decay[j], gnw, mods, prm["ret_w_out"], T, hp=RET_HEADS,
                                     state_in=st_ret, **common)
        else:
            proj, ab = _inproj(x, cidx, mods, prm["norm_w3"], prm["del_w_in"], i, j,
                               "del_conv" if preconv else "del_raw", wab=prm["del_w_ab"], convw=del_conv_w)
            dnw = del_norm_w[j][None, :]
            common = dict(layer=i, j=j, latent=latent, preconv=preconv, fnw=fnw)
            if latent:
                x = _delta_mix(x, proj, ab, del_a_log[j], del_dt_bias[j], del_conv_w, dnw, mods, prm["del_w_out"], T,
                               hp=1, waves=1, s0=state_delta, **common)
            else:
                x, st_del = _delta_mix(x, proj, ab, del_a_log[j], del_dt_bias[j], del_conv_w, dnw, mods,
                                       prm["del_w_out"], T, hp=8, waves=2, state_in=st_del, **common)
    return x, st_ret, st_del


def kernel(x_prompt, x_sample, state_ret, state_delta, c, c_ctx, norm_w, mod_w, mod_b, ret_w_in, ret_decay,
           ret_gn_w, ret_w_out, del_w_in, del_conv_w, del_a_log, del_dt_bias, del_norm_w, del_w_out,
           final_norm_w):
    B, T_ctx, _ = x_prompt.shape
    Bd, T_lat, _ = x_sample.shape
    n_ctx = B * T_ctx
    n_lat = Bd * T_lat

    cond8 = jnp.zeros((8, D_MODEL), F32).at[0].set(c_ctx).at[1:1 + Bd].set(c)
    mods = _modulation(cond8, mod_w, mod_b)
    cidx_ctx = jnp.zeros((n_ctx // ROW_BLOCK,), jnp.int32)
    cidx_lat = 1 + jnp.arange(n_lat // ROW_BLOCK, dtype=jnp.int32) // (T_lat // ROW_BLOCK)
    prm = _prepare_params(norm_w, ret_w_in, ret_w_out, del_w_in, del_w_out, final_norm_w)
    shared = (mods, prm, ret_decay, ret_gn_w, del_conv_w, del_a_log, del_dt_bias, del_norm_w)

    y_ctx, st_ret, st_del = _group_trunk(x_prompt.reshape(n_ctx, D_MODEL), cidx_ctx, T_ctx, False, *shared)
    y_lat, _, _ = _group_trunk(x_sample.reshape(n_lat, D_MODEL), cidx_lat, T_lat, True, *shared,
                               state_ret=state_ret, state_delta=state_delta)
    return (y_ctx.reshape(B, T_ctx, D_MODEL), y_lat.reshape(Bd, T_lat, D_MODEL), st_ret, st_del)
```

```python
import functools

import jax
import jax.numpy as jnp
from jax import lax
from jax.experimental import pallas as pl
from jax.experimental.pallas import tpu as pltpu

F32 = jnp.float32
BF16 = jnp.bfloat16

D_MODEL = 1024
DEPTH = 4
GRID_W = 64
EPS = 1e-6
ROPE_BASE = 10000.0
CONV_W = 3

RET_HEADS = 4
RET_DK = 256
RET_DV = 512
RET_QK = RET_HEADS * RET_DK
RET_V = RET_HEADS * RET_DV
RET_IN = 2 * RET_QK + 2 * RET_V

DEL_HEADS = 8
DEL_DK = 128
DEL_DV = 256
DEL_QK = DEL_HEADS * DEL_DK
DEL_V = DEL_HEADS * DEL_DV
DEL_CONV = 2 * DEL_QK + DEL_V
DEL_MAIN = DEL_CONV + DEL_V
DEL_AB = 4 * DEL_HEADS
assert RET_IN == DEL_MAIN

ROW_BLOCK = 256
COL_GROUP = 512
SUPER = 256
N_LEVELS = SUPER.bit_length() - 1
LANES = 128
SUBLANES = 8
NEG_LOG2_E = -1.4426950408889634
BF16_ROWS = 16
VMEM_LIMIT = 56 * 1024 * 1024


def _sigmoid(x):
    return 1.0 / (1.0 + jnp.exp2(x * NEG_LOG2_E))


def _silu(x):
    return x * _sigmoid(x)


def _softplus(x):
    return jnp.maximum(x, 0.0) + jnp.log1p(jnp.exp(-jnp.abs(x)))


def _log_sigmoid(x):
    return jnp.minimum(x, 0.0) - jnp.log1p(jnp.exp(-jnp.abs(x)))


def _dot(a, b):
    return jnp.dot(a, b, preferred_element_type=F32)


def _dot_nt(a, b):
    return lax.dot_general(a, b, (((1,), (1,)), ((), ())), preferred_element_type=F32)


def _dot_tn(a, b):
    return lax.dot_general(a, b, (((0,), (0,)), ((), ())), preferred_element_type=F32)


def _short_conv(x, w, T):
    y = w[0:1, :] * pltpu.roll(x, 1, 0) + w[1:2, :] * x + w[2:3, :] * pltpu.roll(x, T - 1, 0)
    r8 = lax.broadcasted_iota(jnp.int32, (SUBLANES, 1), 0)
    head = y[:SUBLANES] - jnp.where(r8 == 0, w[0:1, :] * x[T - 1:T, :], 0.0)
    tail = y[T - SUBLANES:] - jnp.where(r8 == SUBLANES - 1, w[2:3, :] * x[0:1, :], 0.0)
    return jnp.concatenate([head, y[SUBLANES:T - SUBLANES], tail], axis=0)


def _l2norm_heads(x, width, scale):
    out = []
    for s in range(0, x.shape[1], width):
        xs = x[:, s:s + width]
        out.append(xs * (lax.rsqrt(jnp.sum(xs * xs, axis=-1, keepdims=True) + EPS) * scale))
    return jnp.concatenate(out, axis=-1)


def _mod_body(cond_ref, w_ref, b_ref, o_ref):
    sc = _silu(cond_ref[...])
    o_ref[...] = jnp.dot(sc, w_ref[...], preferred_element_type=F32,
                         precision=lax.Precision.HIGHEST) + b_ref[...]


def _modulation(cond8, mod_w, mod_b):
    nb = 1
    return pl.pallas_call(
        _mod_body,
        grid=(DEPTH, nb),
        in_specs=[
            pl.BlockSpec((8, D_MODEL), lambda i, j: (0, 0)),
            pl.BlockSpec((None, D_MODEL, 3 * D_MODEL // nb), lambda i, j: (i, 0, j)),
            pl.BlockSpec((None, 1, 3 * D_MODEL // nb), lambda i, j: (i, 0, j)),
        ],
        out_specs=pl.BlockSpec((None, 8, 3 * D_MODEL // nb), lambda i, j: (i, 0, j)),
        out_shape=jax.ShapeDtypeStruct((DEPTH, 8, 3 * D_MODEL), F32),
        compiler_params=pltpu.CompilerParams(vmem_limit_bytes=VMEM_LIMIT),
        name="modulation",
    )(cond8, mod_w, mod_b.reshape(DEPTH, 1, 3 * D_MODEL))


def _inproj_block(ci, x_ref, mod_ref, nw_ref, w_ref, o_ref, mode, cos_ref=None, sin_ref=None, wab_ref=None,
                  cw_ref=None, ab_ref=None):
    x = x_ref[...]
    ms = jnp.mean(x * x, axis=-1, keepdims=True)
    xn = x * lax.rsqrt(ms + EPS) * nw_ref[...]
    m = mod_ref[pl.ds(ci, 1), :]
    shift = m[:, :D_MODEL]
    scale = m[:, D_MODEL:2 * D_MODEL]
    h = (xn * (1.0 + scale) + shift).astype(BF16)
    if mode.startswith("del"):
        ab_ref[...] = _dot(h, wab_ref[...])
    gate_start = 2 * RET_QK + RET_V
    for c0 in range(0, RET_IN, COL_GROUP):
        cols = slice(c0, c0 + COL_GROUP)
        p = _dot(h, w_ref[:, cols])
        if c0 >= gate_start:
            p = _silu(p)
        elif mode == "ret_rope" and c0 < 2 * RET_QK:
            cos = cos_ref[...]
            sin = sin_ref[...]
            hk = RET_DK // 2
            out = []
            for s in range(0, COL_GROUP, RET_DK):
                x1, x2 = p[:, s:s + hk], p[:, s + hk:s + RET_DK]
                out += [x1 * cos - x2 * sin, x1 * sin + x2 * cos]
            p = jnp.concatenate(out, axis=-1)
        elif mode == "del_conv" and c0 < 2 * DEL_QK:
            p = _silu(_short_conv(p, cw_ref[:, cols], ROW_BLOCK))
            p = _l2norm_heads(p, DEL_DK, DEL_DK ** -0.5 if c0 < DEL_QK else 1.0)
        o_ref[:, cols] = p.astype(BF16)


def _inproj_body(cidx_ref, xc_ref, xl_ref, mod_ref, nw_ref, w_ref, *rest, delta, conv_ctx, ctx_blocks):
    i = pl.program_id(0)
    ci = cidx_ref[i]
    if delta:
        wab_ref, cw_ref, o_ref, ab_ref = rest
        extra = dict(wab_ref=wab_ref, cw_ref=cw_ref, ab_ref=ab_ref)
        mode_ctx, mode_lat = ("del_conv" if conv_ctx else "del_raw"), "del_raw"
    else:
        cos_ref, sin_ref, o_ref = rest
        extra = dict(cos_ref=cos_ref, sin_ref=sin_ref)
        mode_ctx, mode_lat = "ret", "ret_rope"

    @pl.when(i < ctx_blocks)
    def _context():
        _inproj_block(ci, xc_ref, mod_ref, nw_ref, w_ref, o_ref, mode_ctx, **extra)

    @pl.when(i >= ctx_blocks)
    def _latent():
        _inproj_block(ci, xl_ref, mod_ref, nw_ref, w_ref, o_ref, mode_lat, **extra)


def _inproj(x_ctx, x_lat, cidx, mods, norm_w3, w, layer, j, *, delta, conv_ctx=False, wab=None, convw=None,
            rope=None, blocks_per_seq=1):
    cb = x_ctx.shape[0] // ROW_BLOCK
    n = x_ctx.shape[0] + x_lat.shape[0]
    ncols = w.shape[2]
    in_specs = [
        pl.BlockSpec((ROW_BLOCK, D_MODEL), lambda i, c: (jnp.minimum(i, cb - 1), 0)),
        pl.BlockSpec((ROW_BLOCK, D_MODEL), lambda i, c: (jnp.maximum(i - cb, 0), 0)),
        pl.BlockSpec((None, 8, 3 * D_MODEL), lambda i, c: (layer, 0, 0)),
        pl.BlockSpec((None, 1, D_MODEL), lambda i, c: (layer, 0, 0)),
        pl.BlockSpec((None, D_MODEL, ncols), lambda i, c: (j, 0, 0)),
    ]
    out_specs = [pl.BlockSpec((ROW_BLOCK, RET_IN), lambda i, c: (i, 0))]
    out_shape = [jax.ShapeDtypeStruct((n, RET_IN), BF16)]
    args = [x_ctx, x_lat, mods, norm_w3, w]
    if delta:
        in_specs += [pl.BlockSpec((None, D_MODEL, LANES), lambda i, c: (j, 0, 0)),
                     pl.BlockSpec((None, CONV_W, DEL_CONV), lambda i, c: (j, 0, 0))]
        args += [wab, convw]
        out_specs.append(pl.BlockSpec((ROW_BLOCK, LANES), lambda i, c: (i, 0)))
        out_shape.append(jax.ShapeDtypeStruct((n, LANES), F32))
    else:
        cos, sin = rope
        spec = pl.BlockSpec((ROW_BLOCK, LANES), lambda i, c: (jnp.maximum(i - cb, 0) % blocks_per_seq, 0))
        in_specs += [spec, spec]
        args += [cos, sin]
    res = pl.pallas_call(
        functools.partial(_inproj_body, delta=delta, conv_ctx=conv_ctx, ctx_blocks=cb),
        grid_spec=pltpu.PrefetchScalarGridSpec(
            num_scalar_prefetch=1, grid=(n // ROW_BLOCK,), in_specs=in_specs, out_specs=out_specs),
        out_shape=out_shape,
        compiler_params=pltpu.CompilerParams(vmem_limit_bytes=VMEM_LIMIT),
        name="inproj_del" if delta else "inproj_ret",
    )(cidx, *args)
    return res if delta else res[0]


def _gated_residual(x_ref, mod_ref, cond_row, wout_ref, oall_s, fnw_ref, y_ref):
    ng, _, width = oall_s.shape
    acc = _dot(oall_s[0], wout_ref[0:width, :])
    for g in range(1, ng):
        acc = acc + _dot(oall_s[g], wout_ref[g * width:(g + 1) * width, :])
    gate = mod_ref[pl.ds(cond_row, 1), :][:, 2 * D_MODEL:]
    xn = x_ref[...] + gate * acc
    if fnw_ref is not None:
        ms = jnp.mean(xn * xn, axis=-1, keepdims=True)
        xn = xn * lax.rsqrt(ms + EPS) * fnw_ref[...]
    y_ref[...] = xn


def _residual_specs(T, vdim, layer, j, final):
    specs = [
        pl.BlockSpec((T, D_MODEL), lambda b, h: (b, 0)),
        pl.BlockSpec((None, 8, 3 * D_MODEL), lambda b, h: (layer, 0, 0)),
        pl.BlockSpec((None, vdim, D_MODEL), lambda b, h: (j, 0, 0)),
    ]
    if final:
        specs.append(pl.BlockSpec((1, D_MODEL), lambda b, h: (0, 0)))
    return specs


def _ret_body(decay_ref, q_ref, k_ref, v_ref, z_ref, gnw_ref, x_ref, mod_ref, wout_ref, *rest, T, latent, final,
              cache_masks, hp):
    fnw_ref = None
    if final:
        fnw_ref, rest = rest[0], rest[1:]
    if latent:
        s0_ref, y_ref, m_s, oall_s = rest
    else:
        y_ref, st_ref, m_s, oall_s = rest
    b = pl.program_id(0)
    scale = RET_DK ** -0.5
    QT = 256
    tcol = lax.broadcasted_iota(jnp.int32, (T, 1), 0).astype(F32)

    heads = []
    for hh in range(hp):
        hd = pl.program_id(1) * hp + hh
        lgf = _log_sigmoid(jnp.full((1, 1), decay_ref[0, hd], F32))
        lgb = _log_sigmoid(jnp.full((1, 1), decay_ref[1, hd], F32))
        mi = hd if cache_masks else 0

        def _build_mask(lgf=lgf, lgb=lgb, mi=mi):
            for qi in range(T // QT):
                ti = lax.broadcasted_iota(jnp.int32, (QT, T), 0) + qi * QT
                tj = lax.broadcasted_iota(jnp.int32, (QT, T), 1)
                dd = (ti - tj).astype(F32)
                lg = jnp.where(dd > 0, lgf, lgb)
                m_s[mi, qi * QT:(qi + 1) * QT, :] = jnp.where(dd == 0, 2.0 * scale,
                                                               jnp.exp(lg * jnp.abs(dd)) * scale)

        if cache_masks:
            pl.when(b == 0)(_build_mask)
        else:
            _build_mask()
        heads.append(dict(hh=hh, lgf=lgf, lgb=lgb, mi=mi,
                          cq=slice(hh * RET_DK, (hh + 1) * RET_DK), cv=slice(hh * RET_DV, (hh + 1) * RET_DV)))

    for qi in range(T // QT):
        rows = slice(qi * QT, (qi + 1) * QT)
        ss = [_dot_nt(q_ref[rows, hd["cq"]], k_ref[:, hd["cq"]]) * m_s[hd["mi"], rows, :] for hd in heads]
        os = [_dot(s.astype(BF16), v_ref[:, hd["cv"]]) for s, hd in zip(ss, heads)]
        if latent:
            tq = tcol[rows]
            for c, hd in enumerate(heads):
                qf = q_ref[rows, hd["cq"]].astype(F32)
                xif = jnp.exp(hd["lgf"] * (tq + 1.0))
                xib = jnp.exp(hd["lgb"] * (float(T) - tq))
                os[c] = (os[c] + _dot((qf * xif).astype(BF16), s0_ref[0, hd["hh"]].astype(BF16))
                         + _dot((qf * xib).astype(BF16), s0_ref[1, hd["hh"]].astype(BF16)))
        for o, hd in zip(os, heads):
            mu = jnp.mean(o, axis=-1, keepdims=True)
            oc = o - mu
            var = jnp.mean(oc * oc, axis=-1, keepdims=True)
            on = oc * lax.rsqrt(var + EPS) * gnw_ref[:, hd["cv"]]
            oall_s[pl.program_id(1), rows, hd["cv"]] = (on * z_ref[rows, hd["cv"]].astype(F32)).astype(BF16)

    if not latent:
        kzs = []
        for hd in heads:
            kf = k_ref[:, hd["cq"]].astype(F32)
            zf = jnp.exp(hd["lgf"] * (float(T - 1) - tcol)) * scale
            zb = jnp.exp(hd["lgb"] * tcol) * scale
            kzs.append(((kf * zf).astype(BF16), (kf * zb).astype(BF16)))
        sts = [(_dot_tn(kz[0], v_ref[:, hd["cv"]]), _dot_tn(kz[1], v_ref[:, hd["cv"]])) for kz, hd in zip(kzs, heads)]
        for st, hd in zip(sts, heads):
            st_ref[0, hd["hh"]] = st[0]
            st_ref[1, hd["hh"]] = st[1]

    @pl.when(pl.program_id(1) == pl.num_programs(1) - 1)
    def _finish_sequence():
        _gated_residual(x_ref, mod_ref, 1 + b if latent else 0, wout_ref, oall_s, fnw_ref, y_ref)


def _ret_mix(x, proj, decay, gnw, mods, w_out, T, *, layer, j, latent, hp, row_off=0, fnw=None, s0=None,
             state_in=None):
    n = x.shape[0]
    nseq = n // T
    final = fnw is not None
    cache_masks = not latent
    H = RET_HEADS
    ng = H // hp
    wk, wv = hp * RET_DK, hp * RET_DV
    smem = pl.BlockSpec(memory_space=pltpu.SMEM)
    in_specs = [
        smem,
        pl.BlockSpec((T, wk), lambda b, h: (b + row_off, h)),
        pl.BlockSpec((T, wk), lambda b, h: (b + row_off, ng + h)),
        pl.BlockSpec((T, wv), lambda b, h: (b + row_off, ng + h)),
        pl.BlockSpec((T, wv), lambda b, h: (b + row_off, 2 * ng + h)),
        pl.BlockSpec((1, wv), lambda b, h: (0, h)),
    ] + _residual_specs(T, RET_V, layer, j, final)
    args = [decay, proj, proj, proj, proj, gnw, x, mods, w_out] + ([fnw] if final else [])
    y_spec = pl.BlockSpec((T, D_MODEL), lambda b, h: (b, 0))
    y_shape = jax.ShapeDtypeStruct((n, D_MODEL), F32)
    st_block = (None, None, 2, hp, RET_DK, RET_DV)
    aliases = {}
    if latent:
        in_specs.append(pl.BlockSpec(st_block, lambda b, h: (b, j, 0, h, 0, 0)))
        args.append(s0)
        out_specs = y_spec
        out_shape = y_shape
    else:
        st_spec = pl.BlockSpec(st_block, lambda b, h: (b, j, 0, h, 0, 0))
        st_shape = jax.ShapeDtypeStruct((nseq, 2, 2, H, RET_DK, RET_DV), F32)
        out_specs = [y_spec, st_spec]
        out_shape = [y_shape, st_shape]
        if state_in is not None:
            in_specs.append(pl.BlockSpec(memory_space=pl.ANY))
            args.append(state_in)
            aliases = {len(args) - 1: 1}
    body = functools.partial(_ret_body, T=T, latent=latent, final=final, cache_masks=cache_masks, hp=hp)
    if state_in is not None:
        inner = body

        def body(*refs):
            n_in = len(args)
            return inner(*refs[:n_in - 1], *refs[n_in:])

    return pl.pallas_call(
        body,
        grid=(nseq, ng),
        in_specs=in_specs,
        out_specs=out_specs,
        out_shape=out_shape,
        scratch_shapes=[
            pltpu.VMEM((H if cache_masks else 1, T, T), F32),
            pltpu.VMEM((ng, T, wv), BF16),
        ],
        input_output_aliases=aliases,
        compiler_params=pltpu.CompilerParams(vmem_limit_bytes=VMEM_LIMIT),
        name="ret_mix_lat" if latent else "ret_mix_ctx",
    )(*args)


def _seg_cumsum(x, pos, axis, reverse):
    n = x.shape[axis]
    s = 1
    while s < SUPER:
        if reverse:
            x = x + jnp.where(pos < SUPER - s, pltpu.roll(x, n - s, axis), 0.0)
        else:
            x = x + jnp.where(pos >= s, pltpu.roll(x, s, axis), 0.0)
        s *= 2
    return x


def _sibling_rows(x, b, sib):
    return jnp.concatenate([x[(2 * m + sib) * b:(2 * m + sib + 1) * b] for m in range(x.shape[0] // (2 * b))], axis=0)


def _tri_inverses(Abs, dirs, eye_s, lm_s, lmh_s, hooks=()):
    n = len(Abs)
    hooks = list(hooks)
    per_level = -(-len(hooks) // (N_LEVELS - 1))
    Xs = [eye_s[...] - Abs[c] * lm_s[dirs[c], 0] for c in range(n)]
    for p in range(1, N_LEVELS):
        b = 1 << p
        if b < BF16_ROWS:
            Ps = [_dot(Xs[c], Abs[c]) for c in range(n)]
            Pm = [Ps[c].astype(BF16) * lm_s[dirs[c], p] for c in range(n)]
            Us = [_dot(Pm[c], Xs[c]) for c in range(n)]
            Xs = [Xs[c] - Us[c].astype(BF16) for c in range(n)]
        else:
            Ps = [_dot(_sibling_rows(Xs[c], b, 1 - dirs[c]), Abs[c]) for c in range(n)]
            Pm = [Ps[c].astype(BF16) * lmh_s[dirs[c], p] for c in range(n)]
            Us = [_dot(Pm[c], Xs[c]).astype(BF16) for c in range(n)]
            for c in range(n):
                d = dirs[c]
                sib = 1 - d
                pieces = []
                for m in range(SUPER // (2 * b)):
                    keep = Xs[c][(2 * m + 1 - sib) * b:(2 * m + 2 - sib) * b]
                    new = Xs[c][(2 * m + sib) * b:(2 * m + sib + 1) * b] - Us[c][m * b:(m + 1) * b]
                    pieces += [new, keep] if d else [keep, new]
                Xs[c] = jnp.concatenate(pieces, axis=0)
        for hook in hooks[(p - 1) * per_level:p * per_level]:
            hook()
    return Xs


def _delta_body(alog_ref, dtb_ref, q_ref, k_ref, v_ref, z_ref, ab_ref, nw_ref, x_ref, mod_ref, wout_ref, *rest,
                T, latent, preconv, final, hp, waves):
    fnw_ref = None
    if final:
        fnw_ref, rest = rest[0], rest[1:]
    cwv_ref, rest = rest[0], rest[1:]
    if not preconv:
        cwq_ref, cwk_ref = rest[:2]
        rest = rest[2:]
    if latent:
        s0_ref, y_ref, eye_s, lm_s, lmh_s, tri_s, gt_s, oacc_s, oall_s = rest
    else:
        y_ref, st_ref, eye_s, lm_s, lmh_s, tri_s, gt_s, oacc_s, oall_s = rest
    nsup = T // SUPER
    sup = [slice(r * SUPER, (r + 1) * SUPER) for r in range(nsup)]

    @pl.when((pl.program_id(0) == 0) & (pl.program_id(1) == 0))
    def _init_masks():
        ii = lax.broadcasted_iota(jnp.int32, (SUPER, SUPER), 0)
        jj = lax.broadcasted_iota(jnp.int32, (SUPER, SUPER), 1)
        eye_s[...] = jnp.where(ii == jj, 1.0, 0.0).astype(BF16)
        xor = ii ^ jj
        lvl = jnp.zeros((SUPER, SUPER), jnp.int32)
        for p in range(1, N_LEVELS):
            lvl = lvl + (xor >= (1 << p)).astype(jnp.int32)
        for d in range(2):
            strict = (ii < jj) if d else (ii > jj)
            for p in range(N_LEVELS):
                m = jnp.where(strict & (lvl == p), 1.0, 0.0).astype(BF16)
                lm_s[d, p] = m
                if (1 << p) >= BF16_ROWS:
                    lmh_s[d, p] = _sibling_rows(m, 1 << p, 1 - d)
            tri_s[d] = jnp.where((ii <= jj) if d else (ii >= jj), 1.0, 0.0).astype(BF16)

    trow = lax.broadcasted_iota(jnp.int32, (T, 1), 0)
    pos_c = trow & (SUPER - 1)
    lane = lax.broadcasted_iota(jnp.int32, (T, LANES), 1)

    ab = ab_ref[...]
    g_all = (NEG_LOG2_E * jnp.exp(alog_ref[...])) * _softplus(ab + dtb_ref[...])
    beta_all = _sigmoid(ab)
    pre = _seg_cumsum(g_all, pos_c, 0, False)
    suf = _seg_cumsum(g_all, pos_c, 0, True)
    G_all = jnp.where(lane >= DEL_HEADS, suf, pre)
    tot_all = pre + suf - g_all
    gt_s[...] = G_all.T

    def column(x, col):
        return jnp.sum(jnp.where(lane == col, x, 0.0), axis=1, keepdims=True)

    def setup_head(hh, dest):
        head = pl.program_id(1) * hp + hh
        cq = slice(hh * DEL_DK, (hh + 1) * DEL_DK)
        cv = slice(hh * DEL_DV, (hh + 1) * DEL_DV)
        v = _silu(_short_conv(v_ref[:, cv].astype(F32), cwv_ref[:, cv], T))
        if preconv:
            qb = q_ref[:, cq]
            kb = k_ref[:, cq]
            q = qb.astype(F32)
            k = kb.astype(F32)
        else:
            q = _silu(_short_conv(q_ref[:, cq].astype(F32), cwq_ref[:, cq], T))
            k = _silu(_short_conv(k_ref[:, cq].astype(F32), cwk_ref[:, cq], T))
            q = _l2norm_heads(q, DEL_DK, DEL_DK ** -0.5)
            k = _l2norm_heads(k, DEL_DK, 1.0)
            qb = q.astype(BF16)
            kb = k.astype(BF16)
        kks = [_dot_nt(kb[rows], kb[rows]) for rows in sup]
        qks = [_dot_nt(qb[rows], kb[rows]) for rows in sup]
        for d in range(2):
            col = d * DEL_HEADS + head
            beta = column(beta_all, 2 * DEL_HEADS + col)
            G = column(G_all, col)
            tot = column(tot_all, col)
            G_row = gt_s[pl.ds(col, 1), :]
            for r, rows in enumerate(sup):
                bt = beta[rows]
                Gc = G[rows]
                E = jnp.exp2(jnp.minimum(Gc - G_row[:, rows], 0.0))
                pr = dict(hh=hh, d=d, r=r, rows=rows, cv=cv,
                          Ab=(bt * kks[r] * E).astype(BF16),
                          qkl=(qks[r] * E).astype(BF16) * tri_s[d],
                          vb=(v[rows] * bt).astype(BF16),
                          ke=(k[rows] * jnp.exp2(tot[rows] - Gc)).astype(BF16))
                if latent:
                    eG = jnp.exp2(Gc)
                    pr.update(kw=(k[rows] * (bt * eG)).astype(BF16), qe=q[rows] * eG,
                              et=jnp.exp2(tot[r * SUPER:r * SUPER + 1]))
                dest.append(pr)

    def apply_stages(probs, Xs):
        ubs = []

        def stage_u():
            ubs.extend(_dot(X, pr["vb"]).astype(BF16) for X, pr in zip(Xs, probs))

        def stage_o():
            o0s = [_dot(pr["qkl"], ub) for pr, ub in zip(probs, ubs)]
            for pr, o0 in zip(probs, o0s):
                if pr["d"]:
                    oacc_s[pr["rows"], pr["cv"]] = oacc_s[pr["rows"], pr["cv"]] + o0
                else:
                    oacc_s[pr["rows"], pr["cv"]] = o0

        def stage_r():
            Rs = [_dot_tn(pr["ke"], ub) for pr, ub in zip(probs, ubs)]
            for pr, R in zip(probs, Rs):
                if latent:
                    pr["R"] = R
                else:
                    st_ref[pr["d"], pr["hh"]] = R

        def stage_w():
            wbs = [_dot(X, pr["kw"]).astype(BF16) for X, pr in zip(Xs, probs)]
            qts = [(pr["qe"] - _dot(pr["qkl"], wb)).astype(BF16) for pr, wb in zip(probs, wbs)]
            kms = [_dot_tn(pr["ke"], wb).astype(BF16) for pr, wb in zip(probs, wbs)]
            for pr, qt, km in zip(probs, qts, kms):
                pr.update(qt=qt, km=km)

        return [stage_u, stage_o, stage_r] + ([stage_w] if latent else [])

    per_wave = hp // waves
    wave_probs = [[] for _ in range(waves)]
    for hh in range(per_wave):
        setup_head(hh, wave_probs[0])
    pending = []
    for wv in range(waves):
        hooks = list(pending)
        if wv + 1 < waves:
            hooks += [functools.partial(setup_head, hh, wave_probs[wv + 1])
                      for hh in range((wv + 1) * per_wave, (wv + 2) * per_wave)]
        probs = wave_probs[wv]
        Xs = _tri_inverses([pr["Ab"] for pr in probs], [pr["d"] for pr in probs], eye_s, lm_s, lmh_s, hooks)
        pending = apply_stages(probs, Xs)
    for thunk in pending:
        thunk()

    if latent:
        by_key = {(pr["hh"], pr["d"], pr["r"]): pr for probs in wave_probs for pr in probs}
        states = {(hh, d): s0_ref[d, hh] for hh in range(hp) for d in range(2)}
        for step in range(nsup):
            for hh in range(hp):
                for d in range(2):
                    pr = by_key[(hh, d, nsup - 1 - step if d else step)]
                    S = states[(hh, d)]
                    Sb = S.astype(BF16)
                    oacc_s[pr["rows"], pr["cv"]] = oacc_s[pr["rows"], pr["cv"]] + _dot(pr["qt"], Sb)
                    states[(hh, d)] = S * pr["et"] + pr["R"] - _dot(pr["km"], Sb)

    for hh in range(hp):
        cv = slice(hh * DEL_DV, (hh + 1) * DEL_DV)
        o = oacc_s[:, cv]
        o = o * lax.rsqrt(jnp.mean(o * o, axis=-1, keepdims=True) + EPS)
        oall_s[pl.program_id(1), :, cv] = (o * nw_ref[:, cv] * z_ref[:, cv].astype(F32)).astype(BF16)

    @pl.when(pl.program_id(1) == pl.num_programs(1) - 1)
    def _finish_sequence():
        _gated_residual(x_ref, mod_ref, 1 + pl.program_id(0) if latent else 0, wout_ref, oall_s, fnw_ref, y_ref)


def _delta_mix(x, proj, ab, alog, dtb, convw, nw, mods, w_out, T, *, layer, j, hp, waves, latent, preconv,
               row_off=0, fnw=None, s0=None, state_in=None):
    n = x.shape[0]
    nseq = n // T
    final = fnw is not None
    H = DEL_HEADS
    ng = H // hp
    wk, wv = hp * DEL_DK, hp * DEL_DV
    alog = jnp.pad(alog.reshape(1, 2 * H), ((0, 0), (0, LANES - 2 * H)))
    dtb = jnp.pad(dtb.reshape(1, 2 * H), ((0, 0), (0, LANES - 2 * H)))
    row = pl.BlockSpec((1, LANES), lambda b, h: (0, 0))
    in_specs = [
        row, row,
        pl.BlockSpec((T, wk), lambda b, h: (b + row_off, h)),
        pl.BlockSpec((T, wk), lambda b, h: (b + row_off, ng + h)),
        pl.BlockSpec((T, wv), lambda b, h: (b + row_off, ng + h)),
        pl.BlockSpec((T, wv), lambda b, h: (b + row_off, 2 * ng + h)),
        pl.BlockSpec((T, LANES), lambda b, h: (b + row_off, 0)),
        pl.BlockSpec((1, wv), lambda b, h: (0, h)),
    ] + _residual_specs(T, DEL_V, layer, j, final)
    args = [alog, dtb, proj, proj, proj, proj, ab, nw, x, mods, w_out] + ([fnw] if final else [])
    in_specs.append(pl.BlockSpec((None, CONV_W, wv), lambda b, h: (j, 0, ng + h)))
    args.append(convw)
    if not preconv:
        in_specs += [
            pl.BlockSpec((None, CONV_W, wk), lambda b, h: (j, 0, h)),
            pl.BlockSpec((None, CONV_W, wk), lambda b, h: (j, 0, ng + h)),
        ]
        args += [convw, convw]
    y_spec = pl.BlockSpec((T, D_MODEL), lambda b, h: (b, 0))
    y_shape = jax.ShapeDtypeStruct((n, D_MODEL), F32)
    st_block = (None, None, 2, hp, DEL_DK, DEL_DV)
    aliases = {}
    scratch = [
        pltpu.VMEM((SUPER, SUPER), BF16),
        pltpu.VMEM((2, N_LEVELS, SUPER, SUPER), BF16),
        pltpu.VMEM((2, N_LEVELS, SUPER // 2, SUPER), BF16),
        pltpu.VMEM((2, SUPER, SUPER), BF16),
        pltpu.VMEM((LANES, T), F32),
        pltpu.VMEM((T, wv), F32),
        pltpu.VMEM((ng, T, wv), BF16),
    ]
    if latent:
        in_specs.append(pl.BlockSpec(st_block, lambda b, h: (b, j, 0, h, 0, 0)))
        args.append(s0)
        out_specs = y_spec
        out_shape = y_shape
    else:
        st_spec = pl.BlockSpec(st_block, lambda b, h: (b, j, 0, h, 0, 0))
        st_shape = jax.ShapeDtypeStruct((nseq, 2, 2, H, DEL_DK, DEL_DV), F32)
        out_specs = [y_spec, st_spec]
        out_shape = [y_shape, st_shape]
        if state_in is not None:
            in_specs.append(pl.BlockSpec(memory_space=pl.ANY))
            args.append(state_in)
            aliases = {len(args) - 1: 1}
    body = functools.partial(_delta_body, T=T, latent=latent, preconv=preconv, final=final, hp=hp, waves=waves)
    if state_in is not None:
        inner = body

        def body(*refs):
            n_in = len(args)
            return inner(*refs[:n_in - 1], *refs[n_in:])

    return pl.pallas_call(
        body,
        grid=(nseq, ng),
        in_specs=in_specs,
        out_specs=out_specs,
        out_shape=out_shape,
        scratch_shapes=scratch,
        input_output_aliases=aliases,
        compiler_params=pltpu.CompilerParams(vmem_limit_bytes=VMEM_LIMIT),
        name="delta_mix_lat" if latent else "delta_mix_ctx",
    )(*args)


def _rope_tables(T):
    rows = T // GRID_W
    r = jnp.broadcast_to(jnp.arange(rows)[:, None], (rows, GRID_W)).reshape(T).astype(F32)
    col = jnp.broadcast_to(jnp.arange(GRID_W)[None, :], (rows, GRID_W)).reshape(T).astype(F32)
    n_pairs = RET_DK // 4
    freqs = ROPE_BASE ** (-jnp.arange(n_pairs, dtype=F32) / n_pairs)
    ang = jnp.concatenate([r[:, None] * freqs, col[:, None] * freqs], -1)
    return jnp.cos(ang), jnp.sin(ang)


def _transpose_cast_body(wt_ref, o_ref):
    o_ref[...] = wt_ref[...].T.astype(BF16)


def _cast_del_w_in(w):
    nl = w.shape[0]
    cb = 512
    wt = jnp.swapaxes(w, 1, 2)
    return pl.pallas_call(
        _transpose_cast_body,
        grid=(nl, DEL_MAIN // cb),
        in_specs=[pl.BlockSpec((None, cb, D_MODEL), lambda l, c: (l, c, 0))],
        out_specs=pl.BlockSpec((None, D_MODEL, cb), lambda l, c: (l, 0, c)),
        out_shape=jax.ShapeDtypeStruct((nl, D_MODEL, DEL_MAIN), BF16),
        compiler_params=pltpu.CompilerParams(vmem_limit_bytes=VMEM_LIMIT),
        name="transpose_cast",
    )(wt)


def _prepare_params(norm_w, ret_w_in, ret_w_out, del_w_in, del_w_out, final_norm_w):
    return dict(
        norm_w3=norm_w[:, None, :],
        ret_w_in=ret_w_in.astype(BF16),
        ret_w_out=ret_w_out.astype(BF16),
        del_w_in=_cast_del_w_in(del_w_in),
        del_w_ab=jnp.pad(del_w_in[:, :, DEL_MAIN:], ((0, 0), (0, 0), (0, LANES - DEL_AB))).astype(BF16),
        del_w_out=del_w_out.astype(BF16),
        fnw=final_norm_w[None, :],
    )


def _trunk(x_ctx, x_lat, T_ctx, T_lat, mods, prm, ret_decay, ret_gn_w, del_conv_w, del_a_log, del_dt_bias, del_norm_w,
           state_ret, state_delta, depth=DEPTH):
    n_ctx, n_lat = x_ctx.shape[0], x_lat.shape[0]
    assert n_ctx % T_lat == 0
    lat_off = n_ctx // T_lat
    bps = T_lat // ROW_BLOCK
    cidx = jnp.concatenate([jnp.zeros((n_ctx // ROW_BLOCK,), jnp.int32),
                            1 + jnp.arange(n_lat // ROW_BLOCK, dtype=jnp.int32) // bps])
    rope = _rope_tables(T_lat)
    conv_ctx = T_ctx == ROW_BLOCK
    st_ret = None
    st_del = None
    for i in range(depth):
        j = i // 2
        fnw = prm["fnw"] if i == depth - 1 else None
        if i % 2 == 0:
            proj = _inproj(x_ctx, x_lat, cidx, mods, prm["norm_w3"], prm["ret_w_in"], i, j, delta=False,
                           rope=rope, blocks_per_seq=bps)
            gnw = ret_gn_w[j][None, :]
            common = dict(layer=i, j=j, fnw=fnw)
            x_lat = _ret_mix(x_lat, proj, ret_decay[j], gnw, mods, prm["ret_w_out"], T_lat, hp=1, latent=True,
                             row_off=lat_off, s0=state_ret, **common)
            x_ctx, st_ret = _ret_mix(x_ctx, proj, ret_decay[j], gnw, mods, prm["ret_w_out"], T_ctx, hp=RET_HEADS,
                                     latent=False, state_in=st_ret, **common)
        else:
            proj, ab = _inproj(x_ctx, x_lat, cidx, mods, prm["norm_w3"], prm["del_w_in"], i, j, delta=True,
                               conv_ctx=conv_ctx, wab=prm["del_w_ab"], convw=del_conv_w)
            dnw = del_norm_w[j][None, :]
            common = dict(layer=i, j=j, fnw=fnw)
            x_lat = _delta_mix(x_lat, proj, ab, del_a_log[j], del_dt_bias[j], del_conv_w, dnw, mods, prm["del_w_out"],
                               T_lat, hp=1, waves=1, latent=True, preconv=False, row_off=lat_off, s0=state_delta,
                               **common)
            x_ctx, st_del = _delta_mix(x_ctx, proj, ab, del_a_log[j], del_dt_bias[j], del_conv_w, dnw, mods,
                                       prm["del_w_out"], T_ctx, hp=8, waves=2, latent=False, preconv=conv_ctx,
                                       state_in=st_del, **common)
    return x_ctx, x_lat, st_ret, st_del


def kernel(x_prompt, x_sample, state_ret, state_delta, c, c_ctx, norm_w, mod_w, mod_b, ret_w_in, ret_decay,
           ret_gn_w, ret_w_out, del_w_in, del_conv_w, del_a_log, del_dt_bias, del_norm_w, del_w_out,
           final_norm_w):
    B, T_ctx, _ = x_prompt.shape
    Bd, T_lat, _ = x_sample.shape

    cond8 = jnp.zeros((8, D_MODEL), F32).at[0].set(c_ctx).at[1:1 + Bd].set(c)
    mods = _modulation(cond8, mod_w, mod_b)
    prm = _prepare_params(norm_w, ret_w_in, ret_w_out, del_w_in, del_w_out, final_norm_w)
    y_ctx, y_lat, st_ret, st_del = _trunk(
        x_prompt.reshape(B * T_ctx, D_MODEL), x_sample.reshape(Bd * T_lat, D_MODEL), T_ctx, T_lat, mods, prm,
        ret_decay, ret_gn_w, del_conv_w, del_a_log, del_dt_bias, del_norm_w, state_ret, state_delta)
    return (y_ctx.reshape(B, T_ctx, D_MODEL), y_lat.reshape(Bd, T_lat, D_MODEL), st_ret, st_del)
```
